```python
import math
import jax, jax.numpy as jnp
from jax import lax
import numpy as np

D_MODEL = 2048
BATCH = 4
SEQ = 4096
DEPTH = 2
DEC_BATCH = 32
DEC_SEQ = 16
PAST_LEN = 4096

CHUNK = 64
N_META = 16
N_EVEN = (DEPTH + 1) // 2
N_ODD = DEPTH // 2
RMS_EPS = 1e-6
FFN_RESIDUAL = 0.5
FFN_DIM = 256 * ((8 * D_MODEL // 3 + 255) // 256)
AB_WIDTH = D_MODEL
POOL_WIDTH = AB_WIDTH // 2
POOL_WINDOWS = (2, 4, 8, 16)
POOL_GROUP = POOL_WIDTH // len(POOL_WINDOWS)
POOL_HIST = max(POOL_WINDOWS) - 1
SSM_WIDTH = AB_WIDTH - POOL_WIDTH
SSM_GROUP = 16
SSM_GROUPS = SSM_WIDTH // SSM_GROUP
SSM_STATE = 64
SSM_BLOCK = CHUNK
SSM_DT_MIN = 1e-3
SSM_DT_MAX = 1e-1
SB_HEADS = 16
SB_HEAD_DIM = D_MODEL // SB_HEADS
SB_BLOCK = 128

kernel_name = 'streaming_pool_s5_stickbreak_macaron'


def rmsnorm(x, g):
    xf = x.astype(jnp.float32)
    xf = xf * lax.rsqrt(jnp.mean(xf * xf, axis=-1, keepdims=True) + RMS_EPS)
    return (xf * g.astype(jnp.float32)).astype(x.dtype)


def swiglu(x, w_gate, w_up, w_down):
    return ((jax.nn.silu(x @ w_gate) * (x @ w_up)) @ w_down).astype(x.dtype)


def pool_mixer(u, hist, w_pool, scale):
    L = u.shape[1]
    ext = u if hist is None else jnp.concatenate([hist.astype(u.dtype), u], axis=1)
    T = ext.shape[1]
    H = T - L
    ef = ext.astype(jnp.float32)
    csum = jnp.concatenate([jnp.zeros_like(ef[:, :1]), jnp.cumsum(ef, axis=1)], axis=1)
    end = jnp.arange(H, T) + 1
    outs = []
    for g, w in enumerate(POOL_WINDOWS):
        sl = slice(g * POOL_GROUP, (g + 1) * POOL_GROUP)
        start = jnp.maximum(end - w, 0)
        cnt = (end - start).astype(jnp.float32)
        mean = (csum[:, end, sl] - csum[:, start, sl]) / cnt[None, :, None]
        outs.append(jnp.einsum('blc,cd->bld', mean - ef[:, H:, sl], w_pool[g].astype(jnp.float32)))
    y = jnp.concatenate(outs, axis=-1) * scale.astype(jnp.float32)
    return y.astype(u.dtype), ext[:, T - POOL_HIST:]


def ssm_discretize(a_re, a_im, log_dt, b_re, b_im):
    f32 = jnp.float32
    a_re = a_re.astype(f32); a_im = a_im.astype(f32)
    dt = jnp.exp(log_dt.astype(f32))[:, None]
    mag = jnp.exp(a_re * dt)
    lb_re = mag * jnp.cos(a_im * dt)
    lb_im = mag * jnp.sin(a_im * dt)
    den = a_re * a_re + a_im * a_im
    c_re = ((lb_re - 1.0) * a_re + lb_im * a_im) / den
    c_im = (lb_im * a_re - (lb_re - 1.0) * a_im) / den
    b_re = b_re.astype(f32); b_im = b_im.astype(f32)
    bb_re = c_re[..., None] * b_re - c_im[..., None] * b_im
    bb_im = c_re[..., None] * b_im + c_im[..., None] * b_re
    return lb_re, lb_im, bb_re, bb_im


def complex_affine_combine(e1, e2):
    a1r, a1i, b1r, b1i = e1
    a2r, a2i, b2r, b2i = e2
    return (a1r * a2r - a1i * a2i,
            a1r * a2i + a1i * a2r,
            a2r * b1r - a2i * b1i + b2r,
            a2r * b1i + a2i * b1r + b2i)


def ssm_mixer(u, h0_re, h0_im, a_re, a_im, log_dt, b_re, b_im, c_re, c_im, d, w_glu, b_glu):
    f32 = jnp.float32
    Bn, L, _ = u.shape
    G, P = SSM_GROUPS, SSM_GROUP
    uf = u.astype(f32).reshape(Bn, L, G, P)
    lb_re, lb_im, bb_re, bb_im = ssm_discretize(a_re, a_im, log_dt, b_re, b_im)
    cr = c_re.astype(f32); ci = c_im.astype(f32)
    nblk = -(-L // SSM_BLOCK)
    pad = nblk * SSM_BLOCK - L
    ub = jnp.pad(uf, ((0, 0), (pad, 0), (0, 0), (0, 0))).reshape(Bn, nblk, SSM_BLOCK, G, P).transpose(1, 0, 2, 3, 4)
    valid = (jnp.arange(nblk * SSM_BLOCK) >= pad).reshape(nblk, SSM_BLOCK)

    def body(carry, xs):
        hr0, hi0 = carry
        ublk, vblk = xs
        br = jnp.einsum('btgp,gnp->btgn', ublk, bb_re)
        bi = jnp.einsum('btgp,gnp->btgn', ublk, bb_im)
        m = vblk[None, :, None, None]
        ar = jnp.broadcast_to(jnp.where(m, lb_re, 1.0), br.shape)
        ai = jnp.broadcast_to(jnp.where(m, lb_im, 0.0), br.shape)
        Ar, Ai, Hr, Hi = lax.associative_scan(complex_affine_combine, (ar, ai, br, bi), axis=1)
        hr = Hr + Ar * hr0[:, None] - Ai * hi0[:, None]
        hi = Hi + Ar * hi0[:, None] + Ai * hr0[:, None]
        yblk = jnp.einsum('btgn,gpn->btgp', hr, cr) - jnp.einsum('btgn,gpn->btgp', hi, ci)
        return (hr[:, -1], hi[:, -1]), yblk

    (hr_T, hi_T), ys = lax.scan(body, (h0_re.astype(f32), h0_im.astype(f32)), (ub, valid))
    y = ys.transpose(1, 0, 2, 3, 4).reshape(Bn, nblk * SSM_BLOCK, G, P)[:, pad:]
    y = (y + d.astype(f32).reshape(G, P) * uf).reshape(Bn, L, SSM_WIDTH)
    z = jax.nn.gelu(y)
    out = z * jax.nn.sigmoid(z @ w_glu.astype(f32) + b_glu.astype(f32))
    return out.astype(u.dtype), hr_T, hi_T


def stick_breaking_attention(q, k, v, q_offset):
    f32 = jnp.float32
    Bn, Lq, H, dh = q.shape
    Lk = k.shape[1]
    bq = SB_BLOCK if Lq >= SB_BLOCK else Lq
    nblk = -(-Lq // bq)
    pad = nblk * bq - Lq
    qb_all = jnp.pad(q, ((0, 0), (0, pad), (0, 0), (0, 0))).reshape(Bn, nblk, bq, H, dh).transpose(1, 0, 2, 3, 4)
    kf = k.astype(f32); vf = v.astype(f32)
    key_idx = jnp.arange(Lk)
    scale = 1.0 / math.sqrt(dh)

    def block(args):
        qb, b = args
        q_idx = q_offset + b * bq + jnp.arange(bq)
        mask = key_idx[None, :] < q_idx[:, None]
        z = jnp.einsum('bqhd,bkhd->bhqk', qb.astype(f32), kf) * scale
        log_beta = jax.nn.log_sigmoid(z)
        log_1mb = jnp.where(mask, log_beta - z, 0.0)
        suffix = lax.cumsum(log_1mb, axis=3, reverse=True)
        after = jnp.concatenate([suffix[..., 1:], jnp.zeros_like(suffix[..., :1])], axis=-1)
        w = jnp.where(mask, jnp.exp(log_beta + after), 0.0)
        return jnp.einsum('bhqk,bkhd->bqhd', w, vf)

    out = lax.map(block, (qb_all, jnp.arange(nblk)))
    return out.transpose(1, 0, 2, 3, 4).reshape(Bn, nblk * bq, H, dh)[:, :Lq]


def sb_mixer(h, w_qkv, w_out, k_hist, v_hist):
    Bn, L, _ = h.shape
    qkv = (h @ w_qkv).reshape(Bn, L, 3, SB_HEADS, SB_HEAD_DIM)
    q, k, v = qkv[:, :, 0], qkv[:, :, 1], qkv[:, :, 2]
    if k_hist is None:
        k_all, v_all, off = k, v, 0
    else:
        k_all = jnp.concatenate([k_hist.astype(k.dtype), k], axis=1)
        v_all = jnp.concatenate([v_hist.astype(v.dtype), v], axis=1)
        off = k_hist.shape[1]
    o = stick_breaking_attention(q, k_all, v_all, off).astype(h.dtype)
    return o.reshape(Bn, L, SB_HEADS * SB_HEAD_DIM) @ w_out, k, v


def run_trunk(x, pool_hist, h0_re, h0_im, k_hist, v_hist, p):
    pools, hres, hims, ks, vs = [], [], [], [], []
    for layer in range(DEPTH):
        h = rmsnorm(x, p['ffn_norm'][layer, 0])
        x = x + FFN_RESIDUAL * swiglu(h, p['ffn_w_gate'][layer, 0], p['ffn_w_up'][layer, 0], p['ffn_w_down'][layer, 0])
        h = rmsnorm(x, p['mix_norm'][layer])
        if layer % 2 == 0:
            e = layer // 2
            u = h @ p['ab_w_in'][e]
            ya, tail = pool_mixer(u[..., :POOL_WIDTH], None if pool_hist is None else pool_hist[e],
                                  p['pool_w'][e], p['pool_scale'][e])
            yb, hr, hi = ssm_mixer(u[..., POOL_WIDTH:], h0_re[e], h0_im[e], p['ssm_a_re'][e], p['ssm_a_im'][e],
                                   p['ssm_log_dt'][e], p['ssm_b_re'][e], p['ssm_b_im'][e], p['ssm_c_re'][e],
                                   p['ssm_c_im'][e], p['ssm_d'][e], p['ssm_w_glu'][e], p['ssm_b_glu'][e])
            y = jnp.concatenate([ya, yb], axis=-1) @ p['ab_w_out'][e]
            pools.append(tail); hres.append(hr); hims.append(hi)
        else:
            o = layer // 2
            y, k, v = sb_mixer(h, p['sb_w_qkv'][o], p['sb_w_out'][o],
                               None if k_hist is None else k_hist[o], None if v_hist is None else v_hist[o])
            ks.append(k); vs.append(v)
        x = x + y.astype(x.dtype)
        h = rmsnorm(x, p['ffn_norm'][layer, 1])
        x = x + FFN_RESIDUAL * swiglu(h, p['ffn_w_gate'][layer, 1], p['ffn_w_up'][layer, 1], p['ffn_w_down'][layer, 1])
    x = rmsnorm(x, p['final_norm'])
    return x, jnp.stack(pools), jnp.stack(hres), jnp.stack(hims), jnp.stack(ks), jnp.stack(vs)


def setup_inputs(seed: int = 0) -> dict:
    key = jax.random.key(seed)
    ks = jax.random.split(key, 32)
    f32 = jnp.float32
    G, N, P = SSM_GROUPS, SSM_STATE, SSM_GROUP

    def nrm(k, shape, scale=1.0):
        return jax.random.normal(k, shape, f32) * scale

    return {
        'x_prompt': nrm(ks[0], (BATCH, SEQ, D_MODEL)),
        'x_sample': nrm(ks[1], (DEC_BATCH, DEC_SEQ, D_MODEL)),
        'cache_pool': nrm(ks[2], (N_EVEN, DEC_BATCH, POOL_HIST, POOL_WIDTH)),
        'state_ssm_re': nrm(ks[3], (N_EVEN, DEC_BATCH, G, N), 0.1),
        'state_ssm_im': nrm(ks[4], (N_EVEN, DEC_BATCH, G, N), 0.1),
        'cache_k': nrm(ks[5], (N_ODD, DEC_BATCH, PAST_LEN, SB_HEADS, SB_HEAD_DIM)),
        'cache_v': nrm(ks[6], (N_ODD, DEC_BATCH, PAST_LEN, SB_HEADS, SB_HEAD_DIM)),
        'meta_tokens': nrm(ks[7], (N_META, D_MODEL)),
        'ffn_norm': 1.0 + nrm(ks[8], (DEPTH, 2, D_MODEL), 0.02),
        'ffn_w_gate': nrm(ks[9], (DEPTH, 2, D_MODEL, FFN_DIM), D_MODEL ** -0.5),
        'ffn_w_up': nrm(ks[10], (DEPTH, 2, D_MODEL, FFN_DIM), D_MODEL ** -0.5),
        'ffn_w_down': nrm(ks[11], (DEPTH, 2, FFN_DIM, D_MODEL), FFN_DIM ** -0.5),
        'mix_norm': 1.0 + nrm(ks[12], (DEPTH, D_MODEL), 0.02),
        'ab_w_in': nrm(ks[13], (N_EVEN, D_MODEL, AB_WIDTH), D_MODEL ** -0.5),
        'pool_w': nrm(ks[14], (N_EVEN, len(POOL_WINDOWS), POOL_GROUP, POOL_GROUP), POOL_GROUP ** -0.5),
        'pool_scale': 1.0 + nrm(ks[15], (N_EVEN, POOL_WIDTH), 0.02),
        'ssm_a_re': -0.5 + nrm(ks[16], (N_EVEN, G, N), 0.01),
        'ssm_a_im': jnp.pi * jnp.arange(N, dtype=f32) + nrm(ks[17], (N_EVEN, G, N), 0.01),
        'ssm_log_dt': jax.random.uniform(ks[18], (N_EVEN, G), f32, math.log(SSM_DT_MIN), math.log(SSM_DT_MAX)),
        'ssm_b_re': nrm(ks[19], (N_EVEN, G, N, P), (2 * P) ** -0.5),
        'ssm_b_im': nrm(ks[20], (N_EVEN, G, N, P), (2 * P) ** -0.5),
        'ssm_c_re': nrm(ks[21], (N_EVEN, G, P, N), N ** -0.5),
        'ssm_c_im': nrm(ks[22], (N_EVEN, G, P, N), N ** -0.5),
        'ssm_d': nrm(ks[23], (N_EVEN, SSM_WIDTH)),
        'ssm_w_glu': nrm(ks[24], (N_EVEN, SSM_WIDTH, SSM_WIDTH), SSM_WIDTH ** -0.5),
        'ssm_b_glu': nrm(ks[25], (N_EVEN, SSM_WIDTH), 0.01),
        'ab_w_out': nrm(ks[26], (N_EVEN, AB_WIDTH, D_MODEL), AB_WIDTH ** -0.5),
        'sb_w_qkv': nrm(ks[27], (N_ODD, D_MODEL, 3 * D_MODEL), D_MODEL ** -0.5),
        'sb_w_out': nrm(ks[28], (N_ODD, D_MODEL, D_MODEL), D_MODEL ** -0.5),
        'final_norm': 1.0 + nrm(ks[29], (D_MODEL,), 0.02),
    }


def reference(x_prompt, x_sample, cache_pool, state_ssm_re, state_ssm_im, cache_k, cache_v,
              meta_tokens, ffn_norm, ffn_w_gate, ffn_w_up, ffn_w_down, mix_norm,
              ab_w_in, pool_w, pool_scale, ssm_a_re, ssm_a_im, ssm_log_dt,
              ssm_b_re, ssm_b_im, ssm_c_re, ssm_c_im, ssm_d, ssm_w_glu, ssm_b_glu, ab_w_out,
              sb_w_qkv, sb_w_out, final_norm):
    p = dict(ffn_norm=ffn_norm, ffn_w_gate=ffn_w_gate, ffn_w_up=ffn_w_up, ffn_w_down=ffn_w_down,
             mix_norm=mix_norm, ab_w_in=ab_w_in, pool_w=pool_w, pool_scale=pool_scale,
             ssm_a_re=ssm_a_re, ssm_a_im=ssm_a_im, ssm_log_dt=ssm_log_dt, ssm_b_re=ssm_b_re,
             ssm_b_im=ssm_b_im, ssm_c_re=ssm_c_re, ssm_c_im=ssm_c_im, ssm_d=ssm_d,
             ssm_w_glu=ssm_w_glu, ssm_b_glu=ssm_b_glu, ab_w_out=ab_w_out,
             sb_w_qkv=sb_w_qkv, sb_w_out=sb_w_out, final_norm=final_norm)
    nb = x_prompt.shape[0]
    meta = jnp.broadcast_to(meta_tokens.astype(x_prompt.dtype)[None], (nb, N_META, D_MODEL))
    xp = jnp.concatenate([meta, x_prompt], axis=1)
    h0 = jnp.zeros((N_EVEN, nb, SSM_GROUPS, SSM_STATE), jnp.float32)
    yp, pool_p, re_p, im_p, k_p, v_p = run_trunk(xp, None, h0, h0, None, None, p)
    y_sample, pool_s, re_s, im_s, k_s, v_s = run_trunk(x_sample, cache_pool, state_ssm_re, state_ssm_im,
                                                       cache_k, cache_v, p)
    y_prompt = yp[:, N_META:]
    return (y_prompt, y_sample, pool_p, pool_s, re_p, im_p, re_s, im_s, k_p, v_p, k_s, v_s)
```

```python
import functools
import math

import jax
import jax.numpy as jnp
from jax import lax
from jax.experimental import pallas as pl
from jax.experimental.pallas import tpu as pltpu

f32 = jnp.float32
bf16 = jnp.bfloat16

D_MODEL = 2048
N_META = 16
RMS_EPS = 1e-6
FFN_RESIDUAL = 0.5
FFN_DIM = 5632
POOL_WIDTH = 1024
POOL_WINDOWS = (2, 4, 8, 16)
POOL_GROUP = 256
POOL_HIST = 15
SSM_WIDTH = 1024
SSM_GROUP = 16
SSM_GROUPS = 64
SSM_STATE = 64
SB_HEADS = 16
SB_HEAD_DIM = 128
SB_SCALE = 1.0 / math.sqrt(SB_HEAD_DIM)

CHUNK = 16
PAIRS = SSM_GROUPS // 2
PAIR_IN = 2 * CHUNK * SSM_GROUP
PAIR_STATE = 2 * SSM_STATE

VMEM_LIMIT = 56 * 1024 * 1024
ATTN_TILE = 256
FFN_TILE_F = 512


def _cparams(sem):
    return pltpu.CompilerParams(dimension_semantics=sem, vmem_limit_bytes=VMEM_LIMIT)


def _log2(n):
    assert n & (n - 1) == 0
    return n.bit_length() - 1


def _rms(xf, g):
    ms = jnp.mean(xf * xf, axis=-1, keepdims=True)
    return xf * lax.rsqrt(ms + RMS_EPS) * g


def _ffn_body(*refs, final):
    if final:
        x_ref, g_ref, wg_ref, wu_ref, wd_ref, gf_ref, o_ref, h_ref = refs
    else:
        x_ref, g_ref, wg_ref, wu_ref, wd_ref, o_ref, h_ref = refs
    j = pl.program_id(1)

    @pl.when(j == 0)
    def _():
        xf = x_ref[...]
        h_ref[...] = _rms(xf, g_ref[...]).astype(bf16)
        o_ref[...] = xf

    h = h_ref[...]
    g = jnp.dot(h, wg_ref[...], preferred_element_type=f32)
    u = jnp.dot(h, wu_ref[...], preferred_element_type=f32)
    a = (jax.nn.silu(g) * u * FFN_RESIDUAL).astype(bf16)
    o_ref[...] += jnp.dot(a, wd_ref[...], preferred_element_type=f32)

    if final:
        @pl.when(j == pl.num_programs(1) - 1)
        def _():
            o_ref[...] = _rms(o_ref[...], gf_ref[...])


def _ffn(x, gamma, wg, wu, wd, tm, final_gamma=None):
    rows = x.shape[0]
    final = final_gamma is not None
    in_specs = [
        pl.BlockSpec((tm, D_MODEL), lambda i, j: (i, 0)),
        pl.BlockSpec((1, D_MODEL), lambda i, j: (0, 0)),
        pl.BlockSpec((D_MODEL, FFN_TILE_F), lambda i, j: (0, j)),
        pl.BlockSpec((D_MODEL, FFN_TILE_F), lambda i, j: (0, j)),
        pl.BlockSpec((FFN_TILE_F, D_MODEL), lambda i, j: (j, 0)),
    ]
    args = [x, gamma.reshape(1, D_MODEL), wg, wu, wd]
    if final:
        in_specs.append(pl.BlockSpec((1, D_MODEL), lambda i, j: (0, 0)))
        args.append(final_gamma.reshape(1, D_MODEL))
    return pl.pallas_call(
        functools.partial(_ffn_body, final=final),
        grid=(rows // tm, FFN_DIM // FFN_TILE_F),
        in_specs=in_specs,
        out_specs=pl.BlockSpec((tm, D_MODEL), lambda i, j: (i, 0)),
        out_shape=jax.ShapeDtypeStruct((rows, D_MODEL), f32),
        scratch_shapes=[pltpu.VMEM((tm, D_MODEL), bf16)],
        compiler_params=_cparams(("parallel", "arbitrary")),
        name="ffn_final" if final else "ffn",
    )(*args)


def _norm_proj_body(x_ref, g_ref, w_ref, *refs, with_bf16):
    if with_bf16:
        o_ref, ob_ref, h_ref = refs
    else:
        o_ref, h_ref = refs

    @pl.when(pl.program_id(1) == 0)
    def _():
        h_ref[...] = _rms(x_ref[...], g_ref[...]).astype(bf16)

    r = jnp.dot(h_ref[...], w_ref[...], preferred_element_type=f32)
    o_ref[...] = r
    if with_bf16:
        ob_ref[...] = r.astype(bf16)


def _norm_proj(x, gamma, w, tm, tn, with_bf16):
    rows = x.shape[0]
    n = w.shape[1]
    out_shape = [jax.ShapeDtypeStruct((rows, n), f32)]
    out_specs = [pl.BlockSpec((tm, tn), lambda i, j: (i, j))]
    if with_bf16:
        out_shape.append(jax.ShapeDtypeStruct((rows, n), bf16))
        out_specs.append(pl.BlockSpec((tm, tn), lambda i, j: (i, j)))
    res = pl.pallas_call(
        functools.partial(_norm_proj_body, with_bf16=with_bf16),
        grid=(rows // tm, n // tn),
        in_specs=[
            pl.BlockSpec((tm, D_MODEL), lambda i, j: (i, 0)),
            pl.BlockSpec((1, D_MODEL), lambda i, j: (0, 0)),
            pl.BlockSpec((D_MODEL, tn), lambda i, j: (0, j)),
        ],
        out_specs=out_specs,
        out_shape=out_shape,
        scratch_shapes=[pltpu.VMEM((tm, D_MODEL), bf16)],
        compiler_params=_cparams(("parallel", "arbitrary")),
        name="norm_proj_qkv" if with_bf16 else "norm_proj",
    )(x, gamma.reshape(1, D_MODEL), w)
    return res if with_bf16 else res[0]


def _proj_res_body(a_ref, w_ref, x_ref, o_ref):
    o_ref[...] = x_ref[...] + jnp.dot(a_ref[...], w_ref[...], preferred_element_type=f32)


def _proj_res(a, w, x, tm):
    rows = x.shape[0]
    return pl.pallas_call(
        _proj_res_body,
        grid=(rows // tm,),
        in_specs=[
            pl.BlockSpec((tm, D_MODEL), lambda i: (i, 0)),
            pl.BlockSpec((D_MODEL, D_MODEL), lambda i: (0, 0)),
            pl.BlockSpec((tm, D_MODEL), lambda i: (i, 0)),
        ],
        out_specs=pl.BlockSpec((tm, D_MODEL), lambda i: (i, 0)),
        out_shape=jax.ShapeDtypeStruct((rows, D_MODEL), f32),
        compiler_params=_cparams(("parallel",)),
        name="proj_res",
    )(a, w, x)


def _ssm_body(u_ref, h0r_ref, h0i_ref, m_ref, er_ref, ei_ref, fr_ref, fi_ref, lr_ref, li_ref,
              y_ref, hTr_ref, hTi_ref, dr_ref, di_ref, sr_ref, si_ref, *, rs, nch):
    up = u_ref[0]
    dr_ref[...] = jnp.dot(up, er_ref[0], preferred_element_type=f32)
    di_ref[...] = jnp.dot(up, ei_ref[0], preferred_element_type=f32)
    lr = jnp.broadcast_to(lr_ref[0], (rs, PAIR_STATE))
    li = jnp.broadcast_to(li_ref[0], (rs, PAIR_STATE))

    def step(c, carry):
        hr, hi = carry
        r0 = pl.multiple_of(c * rs, rs)
        sr_ref[pl.ds(r0, rs), :] = hr
        si_ref[pl.ds(r0, rs), :] = hi
        nr = lr * hr - li * hi + dr_ref[pl.ds(r0, rs), :]
        ni = lr * hi + li * hr + di_ref[pl.ds(r0, rs), :]
        return nr, ni

    hr, hi = lax.fori_loop(0, nch, step, (h0r_ref[0], h0i_ref[0]))
    hTr_ref[0] = hr
    hTi_ref[0] = hi

    half = PAIR_IN // 2
    y0 = jnp.dot(up[:, :half], m_ref[0], preferred_element_type=f32)
    y1 = jnp.dot(up[:, half:], m_ref[1], preferred_element_type=f32)
    ys = (jnp.dot(sr_ref[...].astype(bf16), fr_ref[0], preferred_element_type=f32)
          + jnp.dot(si_ref[...].astype(bf16), fi_ref[0], preferred_element_type=f32))
    y_ref[0, :, :half] = y0 + ys[:, :half]
    y_ref[0, :, half:] = y1 + ys[:, half:]


def _ssm_scan(u, h0r, h0i, mats, rs, nch):
    m, er, ei, fr, fi, lr, li = mats
    rows = rs * nch
    pair3 = lambda p: (p, 0, 0)
    return pl.pallas_call(
        functools.partial(_ssm_body, rs=rs, nch=nch),
        grid=(PAIRS,),
        in_specs=[
            pl.BlockSpec((1, rows, PAIR_IN), pair3),
            pl.BlockSpec((1, rs, PAIR_STATE), pair3),
            pl.BlockSpec((1, rs, PAIR_STATE), pair3),
            pl.BlockSpec((2, PAIR_IN // 2, PAIR_IN // 2), pair3),
            pl.BlockSpec((1, PAIR_IN, PAIR_STATE), pair3),
            pl.BlockSpec((1, PAIR_IN, PAIR_STATE), pair3),
            pl.BlockSpec((1, PAIR_STATE, PAIR_IN), pair3),
            pl.BlockSpec((1, PAIR_STATE, PAIR_IN), pair3),
            pl.BlockSpec((1, 1, PAIR_STATE), pair3),
            pl.BlockSpec((1, 1, PAIR_STATE), pair3),
        ],
        out_specs=[
            pl.BlockSpec((1, rows, PAIR_IN), pair3),
            pl.BlockSpec((1, rs, PAIR_STATE), pair3),
            pl.BlockSpec((1, rs, PAIR_STATE), pair3),
        ],
        out_shape=[
            jax.ShapeDtypeStruct((PAIRS, rows, PAIR_IN), f32),
            jax.ShapeDtypeStruct((PAIRS, rs, PAIR_STATE), f32),
            jax.ShapeDtypeStruct((PAIRS, rs, PAIR_STATE), f32),
        ],
        scratch_shapes=[pltpu.VMEM((rows, PAIR_STATE), f32)] * 4,
        compiler_params=_cparams(("parallel",)),
        name="ssm_scan",
    )(u, h0r, h0i, m, er, ei, fr, fi, lr, li)


def _ssm_matrices(a_re, a_im, log_dt, b_re, b_im, c_re, c_im):
    hp = lax.Precision.HIGHEST
    G, N, P, T = SSM_GROUPS, SSM_STATE, SSM_GROUP, CHUNK
    a_re = a_re.astype(f32)
    a_im = a_im.astype(f32)
    dt = jnp.exp(log_dt.astype(f32))[:, None]
    mag = jnp.exp(a_re * dt)
    lb_re = mag * jnp.cos(a_im * dt)
    lb_im = mag * jnp.sin(a_im * dt)
    den = a_re * a_re + a_im * a_im
    k_re = ((lb_re - 1.0) * a_re + lb_im * a_im) / den
    k_im = (lb_im * a_re - (lb_re - 1.0) * a_im) / den
    b_re = b_re.astype(f32)
    b_im = b_im.astype(f32)
    bb_re = k_re[..., None] * b_re - k_im[..., None] * b_im
    bb_im = k_re[..., None] * b_im + k_im[..., None] * b_re
    cr = c_re.astype(f32)
    ci = c_im.astype(f32)

    def pw_step(c, _):
        re, im = c
        return (re * lb_re - im * lb_im, re * lb_im + im * lb_re), (re, im)

    _, (pw_re, pw_im) = lax.scan(pw_step, (jnp.ones_like(lb_re), jnp.zeros_like(lb_re)), None, length=T + 1)

    x_re = cr[:, :, :, None] * bb_re[:, None] - ci[:, :, :, None] * bb_im[:, None]
    x_im = cr[:, :, :, None] * bb_im[:, None] + ci[:, :, :, None] * bb_re[:, None]
    kern = (jnp.einsum('tgn,gqnp->tgqp', pw_re[:T], x_re, precision=hp)
            - jnp.einsum('tgn,gqnp->tgqp', pw_im[:T], x_im, precision=hp))
    s_idx = jnp.arange(T)[:, None]
    t_idx = jnp.arange(T)[None, :]
    lag = t_idx - s_idx
    toep = jnp.where((lag >= 0)[:, :, None, None, None], kern[jnp.clip(lag, 0, T - 1)], 0.0)
    m = toep.transpose(2, 0, 4, 1, 3).reshape(G, T * P, T * P).astype(bf16)

    rev_re = pw_re[:T][::-1]
    rev_im = pw_im[:T][::-1]
    e_re = rev_re[:, :, :, None] * bb_re[None] - rev_im[:, :, :, None] * bb_im[None]
    e_im = rev_re[:, :, :, None] * bb_im[None] + rev_im[:, :, :, None] * bb_re[None]

    def pair_e(e):
        e = e.transpose(1, 0, 3, 2).reshape(PAIRS, 2, T * P, N)
        z = jnp.zeros_like(e[:, 0])
        top = jnp.concatenate([e[:, 0], z], axis=-1)
        bot = jnp.concatenate([z, e[:, 1]], axis=-1)
        return jnp.concatenate([top, bot], axis=1).astype(bf16)

    nx_re = pw_re[1:]
    nx_im = pw_im[1:]
    crt = cr.transpose(0, 2, 1)
    cit = ci.transpose(0, 2, 1)
    f_re = crt[None] * nx_re[:, :, :, None] - cit[None] * nx_im[:, :, :, None]
    f_im = -(crt[None] * nx_im[:, :, :, None] + cit[None] * nx_re[:, :, :, None])

    def pair_f(f):
        f = f.transpose(1, 2, 0, 3).reshape(PAIRS, 2, N, T * P)
        z = jnp.zeros_like(f[:, 0])
        top = jnp.concatenate([f[:, 0], z], axis=-1)
        bot = jnp.concatenate([z, f[:, 1]], axis=-1)
        return jnp.concatenate([top, bot], axis=1).astype(bf16)

    lam_re = pw_re[T].reshape(PAIRS, 1, PAIR_STATE)
    lam_im = pw_im[T].reshape(PAIRS, 1, PAIR_STATE)
    return m, pair_e(e_re), pair_e(e_im), pair_f(f_re), pair_f(f_im), lam_re, lam_im


def _to_pairs(u_ssm, n_streams, n_chunks, pad_streams):
    t = u_ssm.astype(bf16).reshape(n_streams, n_chunks, CHUNK, PAIRS, 2, SSM_GROUP)
    t = t.transpose(3, 1, 0, 4, 2, 5)
    t = jnp.pad(t, ((0, 0), (0, 0), (0, pad_streams - n_streams), (0, 0), (0, 0), (0, 0)))
    return t.reshape(PAIRS, n_chunks * pad_streams, PAIR_IN)


def _from_pairs(y, n_streams, n_chunks, pad_streams):
    t = y.reshape(PAIRS, n_chunks, pad_streams, 2, CHUNK, SSM_GROUP)[:, :, :n_streams]
    t = t.transpose(2, 1, 4, 0, 3, 5)
    return t.reshape(n_streams * n_chunks * CHUNK, SSM_WIDTH)


def _state_to_pairs(h, pad_streams):
    n = h.shape[0]
    t = h.astype(f32).reshape(n, PAIRS, PAIR_STATE).transpose(1, 0, 2)
    return jnp.pad(t, ((0, 0), (0, pad_streams - n), (0, 0)))


def _state_from_pairs(h, n_streams):
    return h[:, :n_streams].transpose(1, 0, 2).reshape(n_streams, SSM_GROUPS, SSM_STATE)


def _mix0_body(u_ref, y_ref, x_ref, hist_ref, p0_ref, pw_ref, ps_ref, d_ref, wglu_ref, bglu_ref, wout_ref,
               o_ref, ext_ref, *, sb, sl):
    rows = sb * sl
    ext_ref[:, :N_META, :] = hist_ref[...]
    ext_ref[:, N_META:, :] = u_ref[:, :, :POOL_WIDTH]
    step = lax.broadcasted_iota(jnp.int32, (sb, sl, POOL_GROUP), 1).astype(f32)
    seen = p0_ref[...] + step + 1.0

    acc = x_ref[...].reshape(rows, D_MODEL)
    for gi, w in enumerate(POOL_WINDOWS):
        cols = slice(gi * POOL_GROUP, (gi + 1) * POOL_GROUP)
        tot = ext_ref[:, N_META:N_META + sl, cols]
        for k in range(1, w):
            tot = tot + ext_ref[:, N_META - k:N_META - k + sl, cols]
        mean = tot / jnp.minimum(seen, float(w))
        diff = (mean - ext_ref[:, N_META:N_META + sl, cols]).reshape(rows, POOL_GROUP)
        ya = jnp.dot(diff.astype(bf16), pw_ref[gi], preferred_element_type=f32) * ps_ref[:, cols]
        acc = acc + jnp.dot(ya.astype(bf16), wout_ref[cols, :], preferred_element_type=f32)

    us = u_ref[:, :, POOL_WIDTH:].reshape(rows, SSM_WIDTH)
    y = y_ref[...].reshape(rows, SSM_WIDTH) + d_ref[...] * us
    z = jax.nn.gelu(y)
    gate = jax.nn.sigmoid(jnp.dot(z.astype(bf16), wglu_ref[...], preferred_element_type=f32) + bglu_ref[...])
    yb = z * gate
    acc = acc + jnp.dot(yb.astype(bf16), wout_ref[POOL_WIDTH:, :], preferred_element_type=f32)
    o_ref[...] = acc.reshape(sb, sl, D_MODEL)


def _mix0(u, y, x, hist, p0, pool_w, pool_scale, ssm_d, w_glu, b_glu, w_out, sb, sl):
    nseg = u.shape[0] // sl
    u3 = u.reshape(nseg, sl, D_MODEL)
    y3 = y.reshape(nseg, sl, SSM_WIDTH)
    x3 = x.reshape(nseg, sl, D_MODEL)
    seg3 = lambda i: (i, 0, 0)
    const2 = lambda i: (0, 0)
    out = pl.pallas_call(
        functools.partial(_mix0_body, sb=sb, sl=sl),
        grid=(nseg // sb,),
        in_specs=[
            pl.BlockSpec((sb, sl, D_MODEL), seg3),
            pl.BlockSpec((sb, sl, SSM_WIDTH), seg3),
            pl.BlockSpec((sb, sl, D_MODEL), seg3),
            pl.BlockSpec((sb, N_META, POOL_WIDTH), seg3),
            pl.BlockSpec((sb, 1, POOL_GROUP), seg3),
            pl.BlockSpec((len(POOL_WINDOWS), POOL_GROUP, POOL_GROUP), lambda i: (0, 0, 0)),
            pl.BlockSpec((1, POOL_WIDTH), const2),
            pl.BlockSpec((1, SSM_WIDTH), const2),
            pl.BlockSpec((SSM_WIDTH, SSM_WIDTH), const2),
            pl.BlockSpec((1, SSM_WIDTH), const2),
            pl.BlockSpec((D_MODEL, D_MODEL), const2),
        ],
        out_specs=pl.BlockSpec((sb, sl, D_MODEL), seg3),
        out_shape=jax.ShapeDtypeStruct((nseg, sl, D_MODEL), f32),
        scratch_shapes=[pltpu.VMEM((sb, N_META + sl, POOL_WIDTH), f32)],
        compiler_params=_cparams(("parallel",)),
        name="mix0",
    )(u3, y3, x3, hist, p0, pool_w, pool_scale.reshape(1, POOL_WIDTH), ssm_d.reshape(1, SSM_WIDTH),
      w_glu, b_glu.reshape(1, SSM_WIDTH), w_out)
    return out.reshape(nseg * sl, D_MODEL)


def _sb_tile(q, kt, vt, tri, carry, acc, mask):
    s = lax.dot_general(q, kt, (((1,), (1,)), ((), ())), preferred_element_type=f32)
    z = s * SB_SCALE
    t = jnp.log1p(jnp.exp(-jnp.abs(z)))
    log_beta = jnp.minimum(z, 0.0) - t
    log_1mb = jnp.minimum(-z, 0.0) - t
    if mask is not None:
        log_1mb = jnp.where(mask, log_1mb, 0.0)
    hi = log_1mb.astype(bf16)
    lo = (log_1mb - hi.astype(f32)).astype(bf16)
    after = (jnp.dot(hi, tri, preferred_element_type=f32)
             + jnp.dot(lo, tri, preferred_element_type=f32) + carry)
    w = jnp.exp(log_beta + after)
    if mask is not None:
        w = jnp.where(mask, w, 0.0)
    acc = acc + jnp.dot(w.astype(bf16), vt, preferred_element_type=f32)
    carry = after[:, :1] + log_1mb[:, :1]
    return carry, acc


def _attn_prompt_body(q_ref, k_ref, v_ref, km_ref, vm_ref, tri_ref, o_ref):
    tq = ATTN_TILE
    qi = pl.program_id(2)
    q = q_ref[...]
    tri = tri_ref[...]
    row = lax.broadcasted_iota(jnp.int32, (tq, tq), 0)
    col = lax.broadcasted_iota(jnp.int32, (tq, tq), 1)
    carry = jnp.zeros((tq, 1), f32)
    acc = jnp.zeros((tq, SB_HEAD_DIM), f32)
    d0 = pl.multiple_of(qi * tq, tq)
    carry, acc = _sb_tile(q, k_ref[pl.ds(d0, tq), :], v_ref[pl.ds(d0, tq), :], tri, carry, acc, col < row)

    def body(it, c):
        r0 = pl.multiple_of((qi - 1 - it) * tq, tq)
        return _sb_tile(q, k_ref[pl.ds(r0, tq), :], v_ref[pl.ds(r0, tq), :], tri, c[0], c[1], None)

    carry, acc = lax.fori_loop(0, qi, body, (carry, acc))
    colm = lax.broadcasted_iota(jnp.int32, (tq, 128), 1)
    carry, acc = _sb_tile(q, km_ref[...], vm_ref[...], tri_ref[:128, :128], carry, acc, colm < N_META)
    o_ref[...] = acc.astype(o_ref.dtype)


def _attn_prompt(qkvb, kmeta, vmeta, tri, nb, seq):
    tq = ATTN_TILE
    nq = seq // tq
    return pl.pallas_call(
        _attn_prompt_body,
        grid=(nb, SB_HEADS, nq),
        in_specs=[
            pl.BlockSpec((tq, SB_HEAD_DIM), lambda b, h, i: (b * nq + i, h)),
            pl.BlockSpec((seq, SB_HEAD_DIM), lambda b, h, i: (b, SB_HEADS + h)),
            pl.BlockSpec((seq, SB_HEAD_DIM), lambda b, h, i: (b, 2 * SB_HEADS + h)),
            pl.BlockSpec((128, SB_HEAD_DIM), lambda b, h, i: (0, h)),
            pl.BlockSpec((128, SB_HEAD_DIM), lambda b, h, i: (0, h)),
            pl.BlockSpec((tq, tq), lambda b, h, i: (0, 0)),
        ],
        out_specs=pl.BlockSpec((tq, SB_HEAD_DIM), lambda b, h, i: (b * nq + i, h)),
        out_shape=jax.ShapeDtypeStruct((nb * seq, D_MODEL), bf16),
        compiler_params=_cparams(("parallel", "parallel", "arbitrary")),
        name="attn_prompt",
    )(qkvb, qkvb, qkvb, kmeta, vmeta, tri)


def _stack_heads(q):
    ln = q.shape[0]
    rows = SB_HEADS * ln
    qt = jnp.concatenate([q] * SB_HEADS, axis=0)
    rh = lax.shift_right_logical(lax.broadcasted_iota(jnp.int32, (rows, D_MODEL), 0), _log2(ln))
    ch = lax.shift_right_logical(lax.broadcasted_iota(jnp.int32, (rows, D_MODEL), 1), _log2(SB_HEAD_DIM))
    return jnp.where(rh == ch, qt, jnp.zeros_like(qt))


def _own_tile(q_ref, ko_ref, vo_ref, tri_ref, ln):
    rows = SB_HEADS * ln
    qbd = _stack_heads(q_ref[...])
    row = lax.broadcasted_iota(jnp.int32, (rows, 128), 0)
    col = lax.broadcasted_iota(jnp.int32, (rows, 128), 1)
    mask = col < (row & (ln - 1))
    carry = jnp.zeros((rows, 1), f32)
    acc = jnp.zeros((rows, D_MODEL), f32)
    carry, acc = _sb_tile(qbd, ko_ref[...], vo_ref[...], tri_ref[:128, :128], carry, acc, mask)
    return qbd, carry, acc


def _write_heads(o_ref, acc, ln):
    for h in range(SB_HEADS):
        cols = slice(h * SB_HEAD_DIM, (h + 1) * SB_HEAD_DIM)
        o_ref[:, cols] = acc[h * ln:(h + 1) * ln, cols].astype(o_ref.dtype)


def _attn_cached_body(q_ref, ko_ref, vo_ref, kc_ref, vc_ref, tri_ref, o_ref, qbd_ref, carry_ref, acc_ref, *, ln, tk):
    kt = pl.program_id(1)

    @pl.when(kt == 0)
    def _():
        qbd, carry, acc = _own_tile(q_ref, ko_ref, vo_ref, tri_ref, ln)
        qbd_ref[...] = qbd
        carry_ref[...] = carry
        acc_ref[...] = acc

    qbd = qbd_ref[...]
    tri = tri_ref[...]
    carry = carry_ref[...]
    acc = acc_ref[...]
    for sub in reversed(range(tk // ATTN_TILE)):
        rows = slice(sub * ATTN_TILE, (sub + 1) * ATTN_TILE)
        carry, acc = _sb_tile(qbd, kc_ref[0, rows, :].astype(bf16), vc_ref[0, rows, :].astype(bf16),
                              tri, carry, acc, None)
    carry_ref[...] = carry
    acc_ref[...] = acc

    @pl.when(kt == pl.num_programs(1) - 1)
    def _():
        _write_heads(o_ref, acc, ln)


def _attn_cached(qkvb, k_own, v_own, k_cache, v_cache, tri, nb, ln, tk):
    past = k_cache.shape[1]
    nkt = past // tk
    rows = SB_HEADS * ln
    return pl.pallas_call(
        functools.partial(_attn_cached_body, ln=ln, tk=tk),
        grid=(nb, nkt),
        in_specs=[
            pl.BlockSpec((ln, D_MODEL), lambda b, t: (b, 0)),
            pl.BlockSpec((128, D_MODEL), lambda b, t: (b, 0)),
            pl.BlockSpec((128, D_MODEL), lambda b, t: (b, 0)),
            pl.BlockSpec((1, tk, D_MODEL), lambda b, t: (b, nkt - 1 - t, 0)),
            pl.BlockSpec((1, tk, D_MODEL), lambda b, t: (b, nkt - 1 - t, 0)),
            pl.BlockSpec((ATTN_TILE, ATTN_TILE), lambda b, t: (0, 0)),
        ],
        out_specs=pl.BlockSpec((ln, D_MODEL), lambda b, t: (b, 0)),
        out_shape=jax.ShapeDtypeStruct((nb * ln, D_MODEL), bf16),
        scratch_shapes=[
            pltpu.VMEM((rows, D_MODEL), bf16),
            pltpu.VMEM((rows, 1), f32),
            pltpu.VMEM((rows, D_MODEL), f32),
        ],
        compiler_params=_cparams(("parallel", "arbitrary")),
        name="attn_cached",
    )(qkvb, k_own, v_own, k_cache, v_cache, tri)


def _attn_own_body(q_ref, ko_ref, vo_ref, tri_ref, o_ref, *, ln):
    _, _, acc = _own_tile(q_ref, ko_ref, vo_ref, tri_ref, ln)
    _write_heads(o_ref, acc, ln)


def _attn_own(q, k_own, v_own, tri, ln):
    full = lambda i: (0, 0)
    return pl.pallas_call(
        functools.partial(_attn_own_body, ln=ln),
        grid=(1,),
        in_specs=[
            pl.BlockSpec((ln, D_MODEL), full),
            pl.BlockSpec((128, D_MODEL), full),
            pl.BlockSpec((128, D_MODEL), full),
            pl.BlockSpec((ATTN_TILE, ATTN_TILE), full),
        ],
        out_specs=pl.BlockSpec((ln, D_MODEL), full),
        out_shape=jax.ShapeDtypeStruct((ln, D_MODEL), bf16),
        compiler_params=_cparams(("arbitrary",)),
        name="attn_own",
    )(q, k_own, v_own, tri)


def kernel(x_prompt, x_sample, cache_pool, state_ssm_re, state_ssm_im, cache_k, cache_v, meta_tokens, ffn_norm, ffn_w_gate, ffn_w_up, ffn_w_down, mix_norm, ab_w_in, pool_w, pool_scale, ssm_a_re, ssm_a_im, ssm_log_dt, ssm_b_re, ssm_b_im, ssm_c_re, ssm_c_im, ssm_d, ssm_w_glu, ssm_b_glu, ab_w_out, sb_w_qkv, sb_w_out, final_norm):
    nb, seq, _ = x_prompt.shape
    db, dl, _ = x_sample.shape
    assert dl == N_META == CHUNK and seq % ATTN_TILE == 0
    n_small = db + 1
    rows_s = n_small * dl
    tm_p = 512

    wg = ffn_w_gate.astype(bf16)
    wu = ffn_w_up.astype(bf16)
    wd = ffn_w_down.astype(bf16)
    w_in = ab_w_in[0].astype(bf16)
    w_out0 = ab_w_out[0].astype(bf16)
    w_pool = pool_w[0].astype(bf16)
    w_glu = ssm_w_glu[0].astype(bf16)
    w_qkv = sb_w_qkv[0].astype(bf16)
    w_out1 = sb_w_out[0].astype(bf16)
    mats = _ssm_matrices(ssm_a_re[0], ssm_a_im[0], ssm_log_dt[0], ssm_b_re[0], ssm_b_im[0],
                         ssm_c_re[0], ssm_c_im[0])
    ti = jnp.arange(ATTN_TILE)
    tri = (ti[:, None] > ti[None, :]).astype(bf16)

    xp = x_prompt.reshape(nb * seq, D_MODEL)
    xs = jnp.concatenate([x_sample.reshape(db * dl, D_MODEL), meta_tokens.astype(f32)], axis=0)

    xp = _ffn(xp, ffn_norm[0, 0], wg[0, 0], wu[0, 0], wd[0, 0], tm_p)
    xs = _ffn(xs, ffn_norm[0, 0], wg[0, 0], wu[0, 0], wd[0, 0], rows_s)
    up = _norm_proj(xp, mix_norm[0], w_in, tm_p, 1024, False)
    us = _norm_proj(xs, mix_norm[0], w_in, rows_s, 1024, False)

    pad_s = -(-n_small // 8) * 8
    zero_state = jnp.zeros((1, SSM_GROUPS, SSM_STATE), f32)
    h0r = _state_to_pairs(jnp.concatenate([state_ssm_re[0], zero_state], axis=0), pad_s)
    h0i = _state_to_pairs(jnp.concatenate([state_ssm_im[0], zero_state], axis=0), pad_s)
    ys_pairs, hsr, hsi = _ssm_scan(_to_pairs(us[:, POOL_WIDTH:], n_small, 1, pad_s), h0r, h0i, mats, pad_s, 1)
    ys = _from_pairs(ys_pairs, n_small, 1, pad_s)
    us3 = us.reshape(n_small, dl, D_MODEL)
    zrow = jnp.zeros((1, 1, POOL_WIDTH), f32)
    hist_s = jnp.concatenate([
        jnp.concatenate([jnp.zeros((db, 1, POOL_WIDTH), f32), cache_pool[0].astype(f32)], axis=1),
        jnp.zeros((1, N_META, POOL_WIDTH), f32)], axis=0)
    p0_s = jnp.concatenate([jnp.full((db, 1, POOL_GROUP), float(N_META), f32),
                            jnp.zeros((1, 1, POOL_GROUP), f32)], axis=0)
    xs = _mix0(us, ys, xs, hist_s, p0_s, w_pool, pool_scale[0], ssm_d[0], w_glu, ssm_b_glu[0], w_out0,
               n_small, dl)

    pad_p = 8
    n_chunks = seq // CHUNK
    meta_r = jnp.broadcast_to(hsr[:, db:db + 1], (PAIRS, pad_p, PAIR_STATE))
    meta_i = jnp.broadcast_to(hsi[:, db:db + 1], (PAIRS, pad_p, PAIR_STATE))
    yp_pairs, hpr, hpi = _ssm_scan(_to_pairs(up[:, POOL_WIDTH:], nb, n_chunks, pad_p), meta_r, meta_i, mats,
                                   pad_p, n_chunks)
    yp = _from_pairs(yp_pairs, nb, n_chunks, pad_p)
    sl_p = 256
    up4 = up.reshape(nb, seq // sl_p, sl_p, D_MODEL)
    meta_tail = jnp.broadcast_to(us3[db:, :, :POOL_WIDTH], (nb, N_META, POOL_WIDTH))
    hist_p = jnp.concatenate([meta_tail[:, None], up4[:, :-1, sl_p - N_META:, :POOL_WIDTH]], axis=1)
    hist_p = hist_p.reshape(nb * (seq // sl_p), N_META, POOL_WIDTH)
    p0_p = jnp.full((nb * (seq // sl_p), 1, POOL_GROUP), float(N_META), f32)
    xp = _mix0(up, yp, xp, hist_p, p0_p, w_pool, pool_scale[0], ssm_d[0], w_glu, ssm_b_glu[0], w_out0, 1, sl_p)

    xp = _ffn(xp, ffn_norm[0, 1], wg[0, 1], wu[0, 1], wd[0, 1], tm_p)
    xs = _ffn(xs, ffn_norm[0, 1], wg[0, 1], wu[0, 1], wd[0, 1], rows_s)

    xp = _ffn(xp, ffn_norm[1, 0], wg[1, 0], wu[1, 0], wd[1, 0], tm_p)
    xs = _ffn(xs, ffn_norm[1, 0], wg[1, 0], wu[1, 0], wd[1, 0], rows_s)
    qkv_p, qkvb_p = _norm_proj(xp, mix_norm[1], w_qkv, tm_p, 1024, True)
    qkv_s, qkvb_s = _norm_proj(xs, mix_norm[1], w_qkv, rows_s, 1024, True)

    kcol = slice(D_MODEL, 2 * D_MODEL)
    vcol = slice(2 * D_MODEL, 3 * D_MODEL)
    pad_keys = lambda a: jnp.pad(a, ((0, 0), (0, 128 - dl), (0, 0))).reshape(-1, D_MODEL)
    kb3 = qkvb_s[:, kcol].reshape(n_small, dl, D_MODEL)
    vb3 = qkvb_s[:, vcol].reshape(n_small, dl, D_MODEL)
    k_own = pad_keys(kb3[:db])
    v_own = pad_keys(vb3[:db])
    k_meta = pad_keys(kb3[db:])
    v_meta = pad_keys(vb3[db:])

    o_sample = _attn_cached(qkvb_s, k_own, v_own, cache_k[0].reshape(db, -1, D_MODEL),
                            cache_v[0].reshape(db, -1, D_MODEL), tri, db, dl, 512)
    o_meta = _attn_own(qkvb_s[db * dl:, :D_MODEL], k_meta, v_meta, tri, dl)
    o_prompt = _attn_prompt(qkvb_p, k_meta, v_meta, tri, nb, seq)
    xp = _proj_res(o_prompt, w_out1, xp, tm_p)
    xs = _proj_res(jnp.concatenate([o_sample, o_meta], axis=0), w_out1, xs, rows_s)

    yp_out = _ffn(xp, ffn_norm[1, 1], wg[1, 1], wu[1, 1], wd[1, 1], tm_p, final_norm)
    ys_out = _ffn(xs, ffn_norm[1, 1], wg[1, 1], wu[1, 1], wd[1, 1], rows_s, final_norm)

    y_prompt = yp_out.reshape(nb, seq, D_MODEL)
    y_sample = ys_out[:db * dl].reshape(db, dl, D_MODEL)
    up3 = up.reshape(nb, seq, D_MODEL)
    pool_p = up3[:, seq - POOL_HIST:, :POOL_WIDTH][None]
    pool_s = us3[:db, dl - POOL_HIST:, :POOL_WIDTH][None]
    re_p = _state_from_pairs(hpr, nb)[None]
    im_p = _state_from_pairs(hpi, nb)[None]
    re_s = _state_from_pairs(hsr, db)[None]
    im_s = _state_from_pairs(hsi, db)[None]
    heads = lambda a, n: a.reshape(n, -1, SB_HEADS, SB_HEAD_DIM)
    k3 = qkv_s[:, kcol].reshape(n_small, dl, D_MODEL)
    v3 = qkv_s[:, vcol].reshape(n_small, dl, D_MODEL)
    with_meta = lambda m, a: jnp.concatenate(
        [jnp.broadcast_to(m, (nb, dl, D_MODEL)), a.reshape(nb, seq, D_MODEL)], axis=1)
    k_p = heads(with_meta(k3[db:], qkv_p[:, kcol]), nb)[None]
    v_p = heads(with_meta(v3[db:], qkv_p[:, vcol]), nb)[None]
    k_s = heads(k3[:db], db)[None]
    v_s = heads(v3[:db], db)[None]
    return (y_prompt, y_sample, pool_p, pool_s, re_p, im_p, re_s, im_s, k_p, v_p, k_s, v_s)
```

```python
import functools
import math

import jax
import jax.numpy as jnp
from jax import lax
from jax.experimental import pallas as pl
from jax.experimental.pallas import tpu as pltpu

f32 = jnp.float32
bf16 = jnp.bfloat16

D_MODEL = 2048
N_META = 16
RMS_EPS = 1e-6
FFN_RESIDUAL = 0.5
FFN_DIM = 5632
POOL_WIDTH = 1024
POOL_WINDOWS = (2, 4, 8, 16)
POOL_GROUP = 256
POOL_HIST = 15
SSM_WIDTH = 1024
SSM_GROUP = 16
SSM_GROUPS = 64
SSM_STATE = 64
SB_HEADS = 16
SB_HEAD_DIM = 128
LOG2E = 1.4426950408889634
SB_SCALE_LOG2 = LOG2E / math.sqrt(SB_HEAD_DIM)

LANES = 128
CHUNK = 16
SLICES = SSM_WIDTH // LANES
GPS = LANES // SSM_GROUP
SLICE_IN = CHUNK * LANES
SLICE_STATE = GPS * SSM_STATE

VMEM_LIMIT = 56 * 1024 * 1024
ATTN_TILE = 256
ATTN_HEADS_PER_STEP = 8
FFN_TILE_F = 512
TM_PROMPT = 512
MIX0_SEG = 256
CACHE_TILE = 1024


def _cparams(sem):
    return pltpu.CompilerParams(dimension_semantics=sem, vmem_limit_bytes=VMEM_LIMIT)


def _log2(n):
    assert n & (n - 1) == 0
    return n.bit_length() - 1


def _rms(xf, g):
    ms = jnp.mean(xf * xf, axis=-1, keepdims=True)
    return xf * lax.rsqrt(ms + RMS_EPS) * g


def _ffn_body(*refs, final):
    if final:
        x_ref, g_ref, wg_ref, wu_ref, wd_ref, gf_ref, o_ref, h_ref = refs
    else:
        x_ref, g_ref, wg_ref, wu_ref, wd_ref, o_ref, h_ref = refs
    j = pl.program_id(1)

    @pl.when(j == 0)
    def _():
        xf = x_ref[...]
        h_ref[...] = _rms(xf, g_ref[...]).astype(bf16)
        o_ref[...] = xf

    h = h_ref[...]
    g = jnp.dot(h, wg_ref[...], preferred_element_type=f32)
    u = jnp.dot(h, wu_ref[...], preferred_element_type=f32)
    a = (jax.nn.silu(g) * u * FFN_RESIDUAL).astype(bf16)
    o_ref[...] += jnp.dot(a, wd_ref[...], preferred_element_type=f32)

    if final:
        @pl.when(j == pl.num_programs(1) - 1)
        def _():
            o_ref[...] = _rms(o_ref[...], gf_ref[...])


def _ffn(x, gamma, wg, wu, wd, tm, final_gamma=None):
    rows = x.shape[0]
    final = final_gamma is not None
    in_specs = [
        pl.BlockSpec((tm, D_MODEL), lambda i, j: (i, 0)),
        pl.BlockSpec((1, D_MODEL), lambda i, j: (0, 0)),
        pl.BlockSpec((D_MODEL, FFN_TILE_F), lambda i, j: (0, j)),
        pl.BlockSpec((D_MODEL, FFN_TILE_F), lambda i, j: (0, j)),
        pl.BlockSpec((FFN_TILE_F, D_MODEL), lambda i, j: (j, 0)),
    ]
    args = [x, gamma.reshape(1, D_MODEL), wg, wu, wd]
    if final:
        in_specs.append(pl.BlockSpec((1, D_MODEL), lambda i, j: (0, 0)))
        args.append(final_gamma.reshape(1, D_MODEL))
    return pl.pallas_call(
        functools.partial(_ffn_body, final=final),
        grid=(rows // tm, FFN_DIM // FFN_TILE_F),
        in_specs=in_specs,
        out_specs=pl.BlockSpec((tm, D_MODEL), lambda i, j: (i, 0)),
        out_shape=jax.ShapeDtypeStruct((rows, D_MODEL), f32),
        scratch_shapes=[pltpu.VMEM((tm, D_MODEL), bf16)],
        compiler_params=_cparams(("parallel", "arbitrary")),
        name="ffn_final" if final else "ffn",
    )(*args)


def _norm_proj_body(x_ref, g_ref, w_ref, *refs, with_bf16):
    if with_bf16:
        o_ref, ob_ref, h_ref = refs
    else:
        o_ref, h_ref = refs

    @pl.when(pl.program_id(1) == 0)
    def _():
        h_ref[...] = _rms(x_ref[...], g_ref[...]).astype(bf16)

    r = jnp.dot(h_ref[...], w_ref[...], preferred_element_type=f32)
    o_ref[...] = r
    if with_bf16:
        ob_ref[...] = r.astype(bf16)


def _norm_proj(x, gamma, w, tm, tn, with_bf16):
    rows = x.shape[0]
    n = w.shape[1]
    out_shape = [jax.ShapeDtypeStruct((rows, n), f32)]
    out_specs = [pl.BlockSpec((tm, tn), lambda i, j: (i, j))]
    if with_bf16:
        out_shape.append(jax.ShapeDtypeStruct((rows, n), bf16))
        out_specs.append(pl.BlockSpec((tm, tn), lambda i, j: (i, j)))
    res = pl.pallas_call(
        functools.partial(_norm_proj_body, with_bf16=with_bf16),
        grid=(rows // tm, n // tn),
        in_specs=[
            pl.BlockSpec((tm, D_MODEL), lambda i, j: (i, 0)),
            pl.BlockSpec((1, D_MODEL), lambda i, j: (0, 0)),
            pl.BlockSpec((D_MODEL, tn), lambda i, j: (0, j)),
        ],
        out_specs=out_specs,
        out_shape=out_shape,
        scratch_shapes=[pltpu.VMEM((tm, D_MODEL), bf16)],
        compiler_params=_cparams(("parallel", "arbitrary")),
        name="norm_proj_qkv" if with_bf16 else "norm_proj",
    )(x, gamma.reshape(1, D_MODEL), w)
    return res if with_bf16 else res[0]


def _proj_res_body(a_ref, w_ref, x_ref, o_ref):
    o_ref[...] = x_ref[...] + jnp.dot(a_ref[...], w_ref[...], preferred_element_type=f32)


def _proj_res(a, w, x, tm):
    rows = x.shape[0]
    return pl.pallas_call(
        _proj_res_body,
        grid=(rows // tm,),
        in_specs=[
            pl.BlockSpec((tm, D_MODEL), lambda i: (i, 0)),
            pl.BlockSpec((D_MODEL, D_MODEL), lambda i: (0, 0)),
            pl.BlockSpec((tm, D_MODEL), lambda i: (i, 0)),
        ],
        out_specs=pl.BlockSpec((tm, D_MODEL), lambda i: (i, 0)),
        out_shape=jax.ShapeDtypeStruct((rows, D_MODEL), f32),
        compiler_params=_cparams(("parallel",)),
        name="proj_res",
    )(a, w, x)


def _ssm_body(u_ref, h0r_ref, h0i_ref, m_ref, er_ref, ei_ref, fr_ref, fi_ref, lr_ref, li_ref,
              y_ref, hTr_ref, hTi_ref, dr_ref, di_ref, sr_ref, si_ref, *, ns, nch):
    nc = ns * nch
    lhs = jnp.concatenate([u_ref[pl.ds(s, nc, stride=CHUNK), :].astype(bf16) for s in range(CHUNK)], axis=1)
    dr_ref[...] = jnp.dot(lhs, er_ref[0], preferred_element_type=f32)
    di_ref[...] = jnp.dot(lhs, ei_ref[0], preferred_element_type=f32)
    lr = lr_ref[0]
    li = li_ref[0]
    h0r = h0r_ref[0, 0]
    h0i = h0i_ref[0, 0]
    if nch == 1:
        sr_ref[...] = h0r
        si_ref[...] = h0i
        hr = lr * h0r - li * h0i + dr_ref[...]
        hi = lr * h0i + li * h0r + di_ref[...]
    else:
        assert ns == 1

        def step(c, carry):
            hr, hi = carry
            sr_ref[pl.ds(c, 1), :] = hr
            si_ref[pl.ds(c, 1), :] = hi
            return (lr * hr - li * hi + dr_ref[pl.ds(c, 1), :], lr * hi + li * hr + di_ref[pl.ds(c, 1), :])

        hr, hi = lax.fori_loop(0, nch, step, (h0r, h0i), unroll=8)
    hTr_ref[0, 0] = hr
    hTi_ref[0, 0] = hi

    y = (jnp.dot(lhs, m_ref[0], preferred_element_type=f32)
         + jnp.dot(sr_ref[...].astype(bf16), fr_ref[0], preferred_element_type=f32)
         + jnp.dot(si_ref[...].astype(bf16), fi_ref[0], preferred_element_type=f32))
    for t in range(CHUNK):
        y_ref[pl.ds(t, nc, stride=CHUNK), :] = y[:, t * LANES:(t + 1) * LANES]


def _ssm_scan(u, h0r, h0i, mats, nb, ns, nch):
    m, er, ei, fr, fi, lr, li = mats
    rows_b = ns * nch * CHUNK
    nc = ns * nch
    w3 = lambda j, b: (j, 0, 0)
    h4 = lambda j, b: (j, 0, 0, 0)
    return pl.pallas_call(
        functools.partial(_ssm_body, ns=ns, nch=nch),
        grid=(SLICES, nb),
        in_specs=[
            pl.BlockSpec((rows_b, LANES), lambda j, b: (b, SLICES + j)),
            pl.BlockSpec((1, 1, ns, SLICE_STATE), h4),
            pl.BlockSpec((1, 1, ns, SLICE_STATE), h4),
            pl.BlockSpec((1, SLICE_IN, SLICE_IN), w3),
            pl.BlockSpec((1, SLICE_IN, SLICE_STATE), w3),
            pl.BlockSpec((1, SLICE_IN, SLICE_STATE), w3),
            pl.BlockSpec((1, SLICE_STATE, SLICE_IN), w3),
            pl.BlockSpec((1, SLICE_STATE, SLICE_IN), w3),
            pl.BlockSpec((1, 1, SLICE_STATE), w3),
            pl.BlockSpec((1, 1, SLICE_STATE), w3),
        ],
        out_specs=[
            pl.BlockSpec((rows_b, LANES), lambda j, b: (b, j)),
            pl.BlockSpec((1, 1, ns, SLICE_STATE), lambda j, b: (j, b, 0, 0)),
            pl.BlockSpec((1, 1, ns, SLICE_STATE), lambda j, b: (j, b, 0, 0)),
        ],
        out_shape=[
            jax.ShapeDtypeStruct((nb * rows_b, SSM_WIDTH), f32),
            jax.ShapeDtypeStruct((SLICES, nb, ns, SLICE_STATE), f32),
            jax.ShapeDtypeStruct((SLICES, nb, ns, SLICE_STATE), f32),
        ],
        scratch_shapes=[pltpu.VMEM((nc, SLICE_STATE), f32)] * 4,
        compiler_params=_cparams(("arbitrary", "arbitrary")),
        name="ssm_scan",
    )(u, h0r, h0i, m, er, ei, fr, fi, lr, li)


def _ssm_matrices(a_re, a_im, log_dt, b_re, b_im, c_re, c_im):
    hp = lax.Precision.HIGHEST
    N, P, T = SSM_STATE, SSM_GROUP, CHUNK
    a_re = a_re.astype(f32)
    a_im = a_im.astype(f32)
    dt = jnp.exp(log_dt.astype(f32))[:, None]
    mag = jnp.exp(a_re * dt)
    lb_re = mag * jnp.cos(a_im * dt)
    lb_im = mag * jnp.sin(a_im * dt)
    den = a_re * a_re + a_im * a_im
    k_re = ((lb_re - 1.0) * a_re + lb_im * a_im) / den
    k_im = (lb_im * a_re - (lb_re - 1.0) * a_im) / den
    b_re = b_re.astype(f32)
    b_im = b_im.astype(f32)
    bb_re = k_re[..., None] * b_re - k_im[..., None] * b_im
    bb_im = k_re[..., None] * b_im + k_im[..., None] * b_re
    cr = c_re.astype(f32)
    ci = c_im.astype(f32)

    def pw_step(c, _):
        re, im = c
        return (re * lb_re - im * lb_im, re * lb_im + im * lb_re), (re, im)

    _, (pw_re, pw_im) = lax.scan(pw_step, (jnp.ones_like(lb_re), jnp.zeros_like(lb_re)), None, length=T + 1)

    eye = jnp.eye(GPS, dtype=f32)
    x_re = cr[:, :, :, None] * bb_re[:, None] - ci[:, :, :, None] * bb_im[:, None]
    x_im = cr[:, :, :, None] * bb_im[:, None] + ci[:, :, :, None] * bb_re[:, None]
    kern = (jnp.einsum('tgn,gqnp->tgqp', pw_re[:T], x_re, precision=hp)
            - jnp.einsum('tgn,gqnp->tgqp', pw_im[:T], x_im, precision=hp))
    lag = jnp.arange(T)[None, :] - jnp.arange(T)[:, None]
    toep = jnp.where((lag >= 0)[:, :, None, None, None], kern[jnp.clip(lag, 0, T - 1)], 0.0)
    toep = toep.reshape(T, T, SLICES, GPS, P, P).transpose(2, 0, 3, 5, 1, 4)
    m = (toep[:, :, :, :, :, None, :] * eye[None, None, :, None, None, :, None])
    m = m.reshape(SLICES, SLICE_IN, SLICE_IN).astype(bf16)

    rev_re = pw_re[:T][::-1]
    rev_im = pw_im[:T][::-1]
    e_re = rev_re[:, :, :, None] * bb_re[None] - rev_im[:, :, :, None] * bb_im[None]
    e_im = rev_re[:, :, :, None] * bb_im[None] + rev_im[:, :, :, None] * bb_re[None]

    def slice_e(e):
        e = e.reshape(T, SLICES, GPS, N, P).transpose(1, 0, 2, 4, 3)
        e = e[:, :, :, :, None, :] * eye[None, None, :, None, :, None]
        return e.reshape(SLICES, SLICE_IN, SLICE_STATE).astype(bf16)

    nx_re = pw_re[1:]
    nx_im = pw_im[1:]
    crt = cr.transpose(0, 2, 1)
    cit = ci.transpose(0, 2, 1)
    f_re = crt[None] * nx_re[:, :, :, None] - cit[None] * nx_im[:, :, :, None]
    f_im = -(crt[None] * nx_im[:, :, :, None] + cit[None] * nx_re[:, :, :, None])

    def slice_f(f):
        f = f.reshape(T, SLICES, GPS, N, P).transpose(1, 2, 3, 0, 4)
        f = f[:, :, :, :, None, :] * eye[None, :, None, None, :, None]
        return f.reshape(SLICES, SLICE_STATE, SLICE_IN).astype(bf16)

    lam_re = pw_re[T].reshape(SLICES, 1, SLICE_STATE)
    lam_im = pw_im[T].reshape(SLICES, 1, SLICE_STATE)
    return m, slice_e(e_re), slice_e(e_im), slice_f(f_re), slice_f(f_im), lam_re, lam_im


def _state_to_slices(h, pad_streams):
    n = h.shape[0]
    t = h.astype(f32).reshape(n, SLICES, SLICE_STATE).transpose(1, 0, 2)
    return jnp.pad(t, ((0, 0), (0, pad_streams - n), (0, 0)))[:, None]


def _state_from_slices(h):
    return h.transpose(1, 0, 2).reshape(h.shape[1], SSM_GROUPS, SSM_STATE)


def _mix0_body(u_ref, y_ref, x_ref, hist_ref, p0_ref, pw_ref, ps_ref, d_ref, wglu_ref, bglu_ref, wout_ref,
               o_ref, ext_ref, *, sb, sl):
    rows = sb * sl
    ext_ref[:, :N_META, :] = hist_ref[...]
    ext_ref[:, N_META:, :] = u_ref[:, :, :POOL_WIDTH]
    step = lax.broadcasted_iota(jnp.int32, (sb, sl, POOL_GROUP), 1).astype(f32)
    seen = p0_ref[...] + step + 1.0

    acc = x_ref[...].reshape(rows, D_MODEL)
    for gi, w in enumerate(POOL_WINDOWS):
        cols = slice(gi * POOL_GROUP, (gi + 1) * POOL_GROUP)
        tot = ext_ref[:, N_META:N_META + sl, cols]
        for k in range(1, w):
            tot = tot + ext_ref[:, N_META - k:N_META - k + sl, cols]
        mean = tot / jnp.minimum(seen, float(w))
        diff = (mean - ext_ref[:, N_META:N_META + sl, cols]).reshape(rows, POOL_GROUP)
        ya = jnp.dot(diff.astype(bf16), pw_ref[gi], preferred_element_type=f32) * ps_ref[:, cols]
        acc = acc + jnp.dot(ya.astype(bf16), wout_ref[cols, :], preferred_element_type=f32)

    us = u_ref[:, :, POOL_WIDTH:].reshape(rows, SSM_WIDTH)
    y = y_ref[...].reshape(rows, SSM_WIDTH) + d_ref[...] * us
    z = jax.nn.gelu(y)
    gate = jax.nn.sigmoid(jnp.dot(z.astype(bf16), wglu_ref[...], preferred_element_type=f32) + bglu_ref[...])
    yb = z * gate
    acc = acc + jnp.dot(yb.astype(bf16), wout_ref[POOL_WIDTH:, :], preferred_element_type=f32)
    o_ref[...] = acc.reshape(sb, sl, D_MODEL)


def _mix0(u, y, x, hist, p0, pool_w, pool_scale, ssm_d, w_glu, b_glu, w_out, sb, sl):
    nseg = u.shape[0] // sl
    u3 = u.reshape(nseg, sl, D_MODEL)
    y3 = y.reshape(nseg, sl, SSM_WIDTH)
    x3 = x.reshape(nseg, sl, D_MODEL)
    seg3 = lambda i: (i, 0, 0)
    const2 = lambda i: (0, 0)
    out = pl.pallas_call(
        functools.partial(_mix0_body, sb=sb, sl=sl),
        grid=(nseg // sb,),
        in_specs=[
            pl.BlockSpec((sb, sl, D_MODEL), seg3),
            pl.BlockSpec((sb, sl, SSM_WIDTH), seg3),
            pl.BlockSpec((sb, sl, D_MODEL), seg3),
            pl.BlockSpec((sb, N_META, POOL_WIDTH), seg3),
            pl.BlockSpec((sb, 1, POOL_GROUP), seg3),
            pl.BlockSpec((len(POOL_WINDOWS), POOL_GROUP, POOL_GROUP), lambda i: (0, 0, 0)),
            pl.BlockSpec((1, POOL_WIDTH), const2),
            pl.BlockSpec((1, SSM_WIDTH), const2),
            pl.BlockSpec((SSM_WIDTH, SSM_WIDTH), const2),
            pl.BlockSpec((1, SSM_WIDTH), const2),
            pl.BlockSpec((D_MODEL, D_MODEL), const2),
        ],
        out_specs=pl.BlockSpec((sb, sl, D_MODEL), seg3),
        out_shape=jax.ShapeDtypeStruct((nseg, sl, D_MODEL), f32),
        scratch_shapes=[pltpu.VMEM((sb, N_META + sl, POOL_WIDTH), f32)],
        compiler_params=_cparams(("parallel",)),
        name="mix0",
    )(u3, y3, x3, hist, p0, pool_w, pool_scale.reshape(1, POOL_WIDTH), ssm_d.reshape(1, SSM_WIDTH),
      w_glu, b_glu.reshape(1, SSM_WIDTH), w_out)
    return out.reshape(nseg * sl, D_MODEL)


def _sb_tiles(qs, kts, vts, tri, carries, accs, mask, chained=False):
    n = len(qs)
    ss = [lax.dot_general(qs[i], kts[i], (((1,), (1,)), ((), ())), preferred_element_type=f32) for i in range(n)]
    log_betas, log_1mbs, his, los = [], [], [], []
    for s in ss:
        z = s * SB_SCALE_LOG2
        neg_abs = pltpu.bitcast(pltpu.bitcast(z, jnp.uint32) | jnp.uint32(0x80000000), f32)
        t = jnp.log(1.0 + jnp.exp2(neg_abs)) * LOG2E
        log_beta = jnp.minimum(z, 0.0) - t
        log_1mb = log_beta - z
        if mask is not None:
            log_1mb = jnp.where(mask, log_1mb, 0.0)
        hi = log_1mb.astype(bf16)
        log_betas.append(log_beta)
        log_1mbs.append(log_1mb)
        his.append(hi)
        los.append((log_1mb - hi.astype(f32)).astype(bf16))
    sums = [jnp.dot(his[i], tri, preferred_element_type=f32) + jnp.dot(los[i], tri, preferred_element_type=f32)
            for i in range(n)]
    if chained:
        carry = carries[0]
        carries = []
        for i in range(n):
            carries.append(carry)
            carry = carry + sums[i][:, :1] + log_1mbs[i][:, :1]
    afters = [sums[i] + carries[i] for i in range(n)]
    ws = []
    for i in range(n):
        w = jnp.exp2(log_betas[i] + afters[i])
        if mask is not None:
            w = jnp.where(mask, w, 0.0)
        ws.append(w.astype(bf16))
    if chained:
        acc = accs[0]
        for i in range(n):
            acc = acc + jnp.dot(ws[i], vts[i], preferred_element_type=f32)
        return [carry], [acc]
    accs = [accs[i] + jnp.dot(ws[i], vts[i], preferred_element_type=f32) for i in range(n)]
    carries = [afters[i][:, :1] + log_1mbs[i][:, :1] for i in range(n)]
    return carries, accs


def _sb_tile(q, kt, vt, tri, carry, acc, mask):
    carries, accs = _sb_tiles([q], [kt], [vt], tri, [carry], [acc], mask)
    return carries[0], accs[0]


def _attn_prompt_body(q_ref, k_ref, v_ref, km_ref, vm_ref, tri_ref, o_ref, *, hb):
    tq = ATTN_TILE
    qi = pl.program_id(2)
    tri = tri_ref[...]
    cols = [slice(h * SB_HEAD_DIM, (h + 1) * SB_HEAD_DIM) for h in range(hb)]
    qs = [q_ref[:, c] for c in cols]

    def sweep(state, r0, mask):
        kts = [k_ref[pl.ds(r0, tq), c] for c in cols]
        vts = [v_ref[pl.ds(r0, tq), c] for c in cols]
        carries, accs = _sb_tiles(qs, kts, vts, tri, state[0], state[1], mask)
        return tuple(carries), tuple(accs)

    row = lax.broadcasted_iota(jnp.int32, (tq, tq), 0)
    col = lax.broadcasted_iota(jnp.int32, (tq, tq), 1)
    state = (tuple(jnp.zeros((tq, 1), f32) for _ in range(hb)),
             tuple(jnp.zeros((tq, SB_HEAD_DIM), f32) for _ in range(hb)))
    state = sweep(state, pl.multiple_of(qi * tq, tq), col < row)
    state = lax.fori_loop(0, qi, lambda it, st: sweep(st, pl.multiple_of((qi - 1 - it) * tq, tq), None), state)
    colm = lax.broadcasted_iota(jnp.int32, (tq, 128), 1)
    _, accs = _sb_tiles(qs, [km_ref[:, c] for c in cols], [vm_ref[:, c] for c in cols], tri_ref[:128, :128],
                        state[0], state[1], colm < N_META)
    for h in range(hb):
        o_ref[:, cols[h]] = accs[h].astype(o_ref.dtype)


def _attn_prompt(qkvb, kmeta, vmeta, tri, nb, seq):
    tq = ATTN_TILE
    nq = seq // tq
    hb = ATTN_HEADS_PER_STEP
    wd = hb * SB_HEAD_DIM
    ng = SB_HEADS // hb
    return pl.pallas_call(
        functools.partial(_attn_prompt_body, hb=hb),
        grid=(nb, ng, nq),
        in_specs=[
            pl.BlockSpec((tq, wd), lambda b, g, i: (b * nq + i, g)),
            pl.BlockSpec((seq, wd), lambda b, g, i: (b, ng + g)),
            pl.BlockSpec((seq, wd), lambda b, g, i: (b, 2 * ng + g)),
            pl.BlockSpec((128, wd), lambda b, g, i: (0, g)),
            pl.BlockSpec((128, wd), lambda b, g, i: (0, g)),
            pl.BlockSpec((tq, tq), lambda b, g, i: (0, 0)),
        ],
        out_specs=pl.BlockSpec((tq, wd), lambda b, g, i: (b * nq + i, g)),
        out_shape=jax.ShapeDtypeStruct((nb * seq, D_MODEL), bf16),
        compiler_params=_cparams(("parallel", "parallel", "arbitrary")),
        name="attn_prompt",
    )(qkvb, qkvb, qkvb, kmeta, vmeta, tri)


def _stack_heads(q):
    ln = q.shape[0]
    rows = SB_HEADS * ln
    qt = jnp.concatenate([q] * SB_HEADS, axis=0)
    rh = lax.shift_right_logical(lax.broadcasted_iota(jnp.int32, (rows, D_MODEL), 0), _log2(ln))
    ch = lax.shift_right_logical(lax.broadcasted_iota(jnp.int32, (rows, D_MODEL), 1), _log2(SB_HEAD_DIM))
    return jnp.where(rh == ch, qt, jnp.zeros_like(qt))


def _own_tile(q_ref, ko_ref, vo_ref, tri_ref, ln):
    rows = SB_HEADS * ln
    qbd = _stack_heads(q_ref[...])
    row = lax.broadcasted_iota(jnp.int32, (rows, 128), 0)
    col = lax.broadcasted_iota(jnp.int32, (rows, 128), 1)
    mask = col < (row & (ln - 1))
    carry = jnp.zeros((rows, 1), f32)
    acc = jnp.zeros((rows, D_MODEL), f32)
    carry, acc = _sb_tile(qbd, ko_ref[...], vo_ref[...], tri_ref[:128, :128], carry, acc, mask)
    return qbd, carry, acc


def _write_heads(o_ref, acc, ln):
    for h in range(SB_HEADS):
        cols = slice(h * SB_HEAD_DIM, (h + 1) * SB_HEAD_DIM)
        o_ref[:, cols] = acc[h * ln:(h + 1) * ln, cols].astype(o_ref.dtype)


def _head_major_tile(c_ref, sub):
    base = sub * ATTN_TILE * SB_HEADS
    return jnp.concatenate(
        [c_ref[pl.ds(base + h, ATTN_TILE, stride=SB_HEADS), :] for h in range(SB_HEADS)], axis=1).astype(bf16)


def _attn_cached_body(q_ref, ko_ref, vo_ref, kc_ref, vc_ref, tri_ref, o_ref, qbd_ref, carry_ref, acc_ref, *, ln, tk):
    kt = pl.program_id(1)

    @pl.when(kt == 0)
    def _():
        qbd, carry, acc = _own_tile(q_ref, ko_ref, vo_ref, tri_ref, ln)
        qbd_ref[...] = qbd
        carry_ref[...] = carry
        acc_ref[...] = acc

    qbd = qbd_ref[...]
    subs = list(reversed(range(tk // ATTN_TILE)))
    carries, accs = _sb_tiles([qbd] * len(subs), [_head_major_tile(kc_ref, s) for s in subs],
                              [_head_major_tile(vc_ref, s) for s in subs], tri_ref[...],
                              [carry_ref[...]], [acc_ref[...]], None, chained=True)
    carry = carries[0]
    acc = accs[0]
    carry_ref[...] = carry
    acc_ref[...] = acc

    @pl.when(kt == pl.num_programs(1) - 1)
    def _():
        _write_heads(o_ref, acc, ln)


def _attn_cached(qkvb, k_own, v_own, k_cache, v_cache, tri, nb, ln, tk):
    past = k_cache.shape[0] // (nb * SB_HEADS)
    nkt = past // tk
    rows = SB_HEADS * ln
    return pl.pallas_call(
        functools.partial(_attn_cached_body, ln=ln, tk=tk),
        grid=(nb, nkt),
        in_specs=[
            pl.BlockSpec((ln, D_MODEL), lambda b, t: (b, 0)),
            pl.BlockSpec((128, D_MODEL), lambda b, t: (b, 0)),
            pl.BlockSpec((128, D_MODEL), lambda b, t: (b, 0)),
            pl.BlockSpec((tk * SB_HEADS, SB_HEAD_DIM), lambda b, t: (b * nkt + nkt - 1 - t, 0)),
            pl.BlockSpec((tk * SB_HEADS, SB_HEAD_DIM), lambda b, t: (b * nkt + nkt - 1 - t, 0)),
            pl.BlockSpec((ATTN_TILE, ATTN_TILE), lambda b, t: (0, 0)),
        ],
        out_specs=pl.BlockSpec((ln, D_MODEL), lambda b, t: (b, 0)),
        out_shape=jax.ShapeDtypeStruct((nb * ln, D_MODEL), bf16),
        scratch_shapes=[
            pltpu.VMEM((rows, D_MODEL), bf16),
            pltpu.VMEM((rows, 1), f32),
            pltpu.VMEM((rows, D_MODEL), f32),
        ],
        compiler_params=_cparams(("parallel", "arbitrary")),
        name="attn_cached",
    )(qkvb, k_own, v_own, k_cache, v_cache, tri)


def _attn_own_body(q_ref, ko_ref, vo_ref, tri_ref, o_ref, *, ln):
    _, _, acc = _own_tile(q_ref, ko_ref, vo_ref, tri_ref, ln)
    _write_heads(o_ref, acc, ln)


def _attn_own(q, k_own, v_own, tri, ln):
    full = lambda i: (0, 0)
    return pl.pallas_call(
        functools.partial(_attn_own_body, ln=ln),
        grid=(1,),
        in_specs=[
            pl.BlockSpec((ln, D_MODEL), full),
            pl.BlockSpec((128, D_MODEL), full),
            pl.BlockSpec((128, D_MODEL), full),
            pl.BlockSpec((ATTN_TILE, ATTN_TILE), full),
        ],
        out_specs=pl.BlockSpec((ln, D_MODEL), full),
        out_shape=jax.ShapeDtypeStruct((ln, D_MODEL), bf16),
        compiler_params=_cparams(("arbitrary",)),
        name="attn_own",
    )(q, k_own, v_own, tri)


def kernel(x_prompt, x_sample, cache_pool, state_ssm_re, state_ssm_im, cache_k, cache_v, meta_tokens, ffn_norm, ffn_w_gate, ffn_w_up, ffn_w_down, mix_norm, ab_w_in, pool_w, pool_scale, ssm_a_re, ssm_a_im, ssm_log_dt, ssm_b_re, ssm_b_im, ssm_c_re, ssm_c_im, ssm_d, ssm_w_glu, ssm_b_glu, ab_w_out, sb_w_qkv, sb_w_out, final_norm):
    nb, seq, _ = x_prompt.shape
    db, dl, _ = x_sample.shape
    assert dl == N_META == CHUNK and seq % ATTN_TILE == 0
    n_small = db + 1
    rows_s = n_small * dl
    tm_p = TM_PROMPT

    def ffn_weights(layer, k):
        return (ffn_norm[layer, k], ffn_w_gate[layer, k].astype(bf16), ffn_w_up[layer, k].astype(bf16),
                ffn_w_down[layer, k].astype(bf16))

    w_in = ab_w_in[0].astype(bf16)
    w_out0 = ab_w_out[0].astype(bf16)
    w_pool = pool_w[0].astype(bf16)
    w_glu = ssm_w_glu[0].astype(bf16)
    w_qkv = sb_w_qkv[0].astype(bf16)
    w_out1 = sb_w_out[0].astype(bf16)
    mats = _ssm_matrices(ssm_a_re[0], ssm_a_im[0], ssm_log_dt[0], ssm_b_re[0], ssm_b_im[0],
                         ssm_c_re[0], ssm_c_im[0])
    ti = jnp.arange(ATTN_TILE)
    tri = (ti[:, None] > ti[None, :]).astype(bf16)

    xp = x_prompt.reshape(nb * seq, D_MODEL)
    xs = jnp.concatenate([x_sample.reshape(db * dl, D_MODEL), meta_tokens.astype(f32)], axis=0)

    fw = ffn_weights(0, 0)
    xp = _ffn(xp, *fw, tm_p)
    xs = _ffn(xs, *fw, rows_s)
    up = _norm_proj(xp, mix_norm[0], w_in, tm_p, 1024, False)
    us = _norm_proj(xs, mix_norm[0], w_in, rows_s, 1024, False)

    pad_s = -(-n_small // 8) * 8
    zero_state = jnp.zeros((1, SSM_GROUPS, SSM_STATE), f32)
    h0r = _state_to_slices(jnp.concatenate([state_ssm_re[0], zero_state], axis=0), pad_s)
    h0i = _state_to_slices(jnp.concatenate([state_ssm_im[0], zero_state], axis=0), pad_s)
    us_pad = jnp.pad(us, ((0, (pad_s - n_small) * dl), (0, 0)))
    ys, hsr, hsi = _ssm_scan(us_pad, h0r, h0i, mats, 1, pad_s, 1)
    ys = ys[:rows_s]
    hsr = hsr[:, 0]
    hsi = hsi[:, 0]
    us3 = us.reshape(n_small, dl, D_MODEL)
    hist_s = jnp.concatenate([
        jnp.concatenate([jnp.zeros((db, 1, POOL_WIDTH), f32), cache_pool[0].astype(f32)], axis=1),
        jnp.zeros((1, N_META, POOL_WIDTH), f32)], axis=0)
    p0_s = jnp.concatenate([jnp.full((db, 1, POOL_GROUP), float(N_META), f32),
                            jnp.zeros((1, 1, POOL_GROUP), f32)], axis=0)
    xs = _mix0(us, ys, xs, hist_s, p0_s, w_pool, pool_scale[0], ssm_d[0], w_glu, ssm_b_glu[0], w_out0,
               n_small, dl)

    meta_r = hsr[:, db].reshape(SLICES, 1, 1, SLICE_STATE)
    meta_i = hsi[:, db].reshape(SLICES, 1, 1, SLICE_STATE)
    yp, hpr, hpi = _ssm_scan(up, meta_r, meta_i, mats, nb, 1, seq // CHUNK)
    sl_p = MIX0_SEG
    up4 = up.reshape(nb, seq // sl_p, sl_p, D_MODEL)
    meta_tail = jnp.broadcast_to(us3[db:, :, :POOL_WIDTH], (nb, N_META, POOL_WIDTH))
    hist_p = jnp.concatenate([meta_tail[:, None], up4[:, :-1, sl_p - N_META:, :POOL_WIDTH]], axis=1)
    hist_p = hist_p.reshape(nb * (seq // sl_p), N_META, POOL_WIDTH)
    p0_p = jnp.full((nb * (seq // sl_p), 1, POOL_GROUP), float(N_META), f32)
    xp = _mix0(up, yp, xp, hist_p, p0_p, w_pool, pool_scale[0], ssm_d[0], w_glu, ssm_b_glu[0], w_out0, 1, sl_p)

    fw = ffn_weights(0, 1)
    xp = _ffn(xp, *fw, tm_p)
    xs = _ffn(xs, *fw, rows_s)

    fw = ffn_weights(1, 0)
    xp = _ffn(xp, *fw, tm_p)
    xs = _ffn(xs, *fw, rows_s)
    qkv_p, qkvb_p = _norm_proj(xp, mix_norm[1], w_qkv, tm_p, 1024, True)
    qkv_s, qkvb_s = _norm_proj(xs, mix_norm[1], w_qkv, rows_s, 1024, True)

    kcol = slice(D_MODEL, 2 * D_MODEL)
    vcol = slice(2 * D_MODEL, 3 * D_MODEL)
    pad_keys = lambda a: jnp.pad(a, ((0, 0), (0, 128 - dl), (0, 0))).reshape(-1, D_MODEL)
    kb3 = qkvb_s[:, kcol].reshape(n_small, dl, D_MODEL)
    vb3 = qkvb_s[:, vcol].reshape(n_small, dl, D_MODEL)
    k_own = pad_keys(kb3[:db])
    v_own = pad_keys(vb3[:db])
    k_meta = pad_keys(kb3[db:])
    v_meta = pad_keys(vb3[db:])

    past = cache_k.shape[2]
    o_sample = _attn_cached(qkvb_s, k_own, v_own, cache_k.reshape(db * past * SB_HEADS, SB_HEAD_DIM),
                            cache_v.reshape(db * past * SB_HEADS, SB_HEAD_DIM), tri, db, dl, CACHE_TILE)
    o_meta = _attn_own(qkvb_s[db * dl:, :D_MODEL], k_meta, v_meta, tri, dl)
    o_prompt = _attn_prompt(qkvb_p, k_meta, v_meta, tri, nb, seq)
    xp = _proj_res(o_prompt, w_out1, xp, tm_p)
    xs = _proj_res(jnp.concatenate([o_sample, o_meta], axis=0), w_out1, xs, rows_s)

    fw = ffn_weights(1, 1)
    yp_out = _ffn(xp, *fw, tm_p, final_norm)
    ys_out = _ffn(xs, *fw, rows_s, final_norm)

    y_prompt = yp_out.reshape(nb, seq, D_MODEL)
    y_sample = ys_out[:db * dl].reshape(db, dl, D_MODEL)
    up3 = up.reshape(nb, seq, D_MODEL)
    pool_p = up3[:, seq - POOL_HIST:, :POOL_WIDTH][None]
    pool_s = us3[:db, dl - POOL_HIST:, :POOL_WIDTH][None]
    re_p = _state_from_slices(hpr[:, :, 0])[None]
    im_p = _state_from_slices(hpi[:, :, 0])[None]
    re_s = _state_from_slices(hsr[:, :db])[None]
    im_s = _state_from_slices(hsi[:, :db])[None]
    heads = lambda a, n: a.reshape(n, -1, SB_HEADS, SB_HEAD_DIM)
    k3 = qkv_s[:, kcol].reshape(n_small, dl, D_MODEL)
    v3 = qkv_s[:, vcol].reshape(n_small, dl, D_MODEL)
    with_meta = lambda m, a: jnp.concatenate(
        [jnp.broadcast_to(m, (nb, dl, D_MODEL)), a.reshape(nb, seq, D_MODEL)], axis=1)
    k_p = heads(with_meta(k3[db:], qkv_p[:, kcol]), nb)[None]
    v_p = heads(with_meta(v3[db:], qkv_p[:, vcol]), nb)[None]
    k_s = heads(k3[:db], db)[None]
    v_s = heads(v3[:db], db)[None]
    return (y_prompt, y_sample, pool_p, pool_s, re_p, im_p, re_s, im_s, k_p, v_p, k_s, v_s)
```

```python
import functools
import math

import jax
import jax.numpy as jnp
from jax import lax
from jax.experimental import pallas as pl
from jax.experimental.pallas import tpu as pltpu

f32 = jnp.float32
bf16 = jnp.bfloat16

D_MODEL = 2048
N_META = 16
RMS_EPS = 1e-6
FFN_RESIDUAL = 0.5
FFN_DIM = 5632
POOL_WIDTH = 1024
POOL_WINDOWS = (2, 4, 8, 16)
POOL_GROUP = 256
POOL_HIST = 15
SSM_WIDTH = 1024
SSM_GROUP = 16
SSM_GROUPS = 64
SSM_STATE = 64
SB_HEADS = 16
SB_HEAD_DIM = 128
LOG2E = 1.4426950408889634
SB_SCALE_LOG2 = LOG2E / math.sqrt(SB_HEAD_DIM)

LANES = 128
CHUNK = 16
SLICES = SSM_WIDTH // LANES
GPS = LANES // SSM_GROUP
SLICE_IN = CHUNK * LANES
SLICE_STATE = GPS * SSM_STATE

VMEM_LIMIT = 56 * 1024 * 1024
ATTN_TILE = 256
ATTN_HEADS_PER_STEP = 8
ATTN_FLAGS = None
FFN_TILE_F = 512
TM_PROMPT = 512
TM_FFN = 1024
MIX0_SEG = 256
CACHE_TILE = 1024


def _cparams(sem, flags=None):
    return pltpu.CompilerParams(dimension_semantics=sem, vmem_limit_bytes=VMEM_LIMIT, flags=flags)


def _log2(n):
    assert n & (n - 1) == 0
    return n.bit_length() - 1


def _rms(xf, g):
    ms = jnp.mean(xf * xf, axis=-1, keepdims=True)
    return xf * lax.rsqrt(ms + RMS_EPS) * g


def _ffn_body(*refs, final):
    if final:
        x_ref, g_ref, wg_ref, wu_ref, wd_ref, gf_ref, o_ref, h_ref = refs
    else:
        x_ref, g_ref, wg_ref, wu_ref, wd_ref, o_ref, h_ref = refs
    j = pl.program_id(1)

    @pl.when(j == 0)
    def _():
        xf = x_ref[...]
        h_ref[...] = _rms(xf, g_ref[...]).astype(bf16)
        o_ref[...] = xf

    h = h_ref[...]
    g = jnp.dot(h, wg_ref[...], preferred_element_type=f32)
    u = jnp.dot(h, wu_ref[...], preferred_element_type=f32)
    a = (jax.nn.silu(g) * u * FFN_RESIDUAL).astype(bf16)
    o_ref[...] += jnp.dot(a, wd_ref[...], preferred_element_type=f32)

    if final:
        @pl.when(j == pl.num_programs(1) - 1)
        def _():
            o_ref[...] = _rms(o_ref[...], gf_ref[...])


def _ffn(x, gamma, wg, wu, wd, which, tm, final_gamma=None):
    rows = x.shape[0]
    final = final_gamma is not None
    la, po = which
    in_specs = [
        pl.BlockSpec((tm, D_MODEL), lambda i, j: (i, 0)),
        pl.BlockSpec((1, D_MODEL), lambda i, j: (0, 0)),
        pl.BlockSpec((None, None, D_MODEL, FFN_TILE_F), lambda i, j: (la, po, 0, j)),
        pl.BlockSpec((None, None, D_MODEL, FFN_TILE_F), lambda i, j: (la, po, 0, j)),
        pl.BlockSpec((None, None, FFN_TILE_F, D_MODEL), lambda i, j: (la, po, j, 0)),
    ]
    args = [x, gamma.reshape(1, D_MODEL), wg, wu, wd]
    if final:
        in_specs.append(pl.BlockSpec((1, D_MODEL), lambda i, j: (0, 0)))
        args.append(final_gamma.reshape(1, D_MODEL))
    return pl.pallas_call(
        functools.partial(_ffn_body, final=final),
        grid=(rows // tm, FFN_DIM // FFN_TILE_F),
        in_specs=in_specs,
        out_specs=pl.BlockSpec((tm, D_MODEL), lambda i, j: (i, 0)),
        out_shape=jax.ShapeDtypeStruct((rows, D_MODEL), f32),
        scratch_shapes=[pltpu.VMEM((tm, D_MODEL), bf16)],
        compiler_params=_cparams(("parallel", "arbitrary")),
        name="ffn_final" if final else "ffn",
    )(*args)


def _norm_proj_body(x_ref, g_ref, w_ref, *refs, with_bf16):
    if with_bf16:
        o_ref, ob_ref, h_ref = refs
    else:
        o_ref, h_ref = refs

    @pl.when(pl.program_id(1) == 0)
    def _():
        h_ref[...] = _rms(x_ref[...], g_ref[...]).astype(bf16)

    r = jnp.dot(h_ref[...], w_ref[...], preferred_element_type=f32)
    o_ref[...] = r
    if with_bf16:
        ob_ref[...] = r.astype(bf16)


def _norm_proj(x, gamma, w, tm, tn, with_bf16):
    rows = x.shape[0]
    n = w.shape[1]
    out_shape = [jax.ShapeDtypeStruct((rows, n), f32)]
    out_specs = [pl.BlockSpec((tm, tn), lambda i, j: (i, j))]
    if with_bf16:
        out_shape.append(jax.ShapeDtypeStruct((rows, n), bf16))
        out_specs.append(pl.BlockSpec((tm, tn), lambda i, j: (i, j)))
    res = pl.pallas_call(
        functools.partial(_norm_proj_body, with_bf16=with_bf16),
        grid=(rows // tm, n // tn),
        in_specs=[
            pl.BlockSpec((tm, D_MODEL), lambda i, j: (i, 0)),
            pl.BlockSpec((1, D_MODEL), lambda i, j: (0, 0)),
            pl.BlockSpec((D_MODEL, tn), lambda i, j: (0, j)),
        ],
        out_specs=out_specs,
        out_shape=out_shape,
        scratch_shapes=[pltpu.VMEM((tm, D_MODEL), bf16)],
        compiler_params=_cparams(("parallel", "arbitrary")),
        name="norm_proj_qkv" if with_bf16 else "norm_proj",
    )(x, gamma.reshape(1, D_MODEL), w)
    return res if with_bf16 else res[0]


def _qkv_prompt_body(x_ref, g_ref, w_ref, mkv_ref, qkvb_ref, kp_ref, vp_ref, h_ref, stage_ref, sem_ref, msem_ref,
                     *, tm, tn, tpb):
    i = pl.program_id(0)
    j = pl.program_id(1)
    last_i = pl.num_programs(0) - 1
    nj = 3 * D_MODEL // tn
    njq = D_MODEL // tn
    hpt = tn // SB_HEAD_DIM

    @pl.when(j == 0)
    def _():
        h_ref[...] = _rms(x_ref[...], g_ref[...]).astype(bf16)

    r = jnp.dot(h_ref[...], w_ref[:, pl.ds(pl.multiple_of(j * tn, tn), tn)], preferred_element_type=f32)
    qkvb_ref[...] = r.astype(bf16)

    b = lax.shift_right_logical(i, _log2(tpb))
    il = i & (tpb - 1)
    t0 = N_META + il * tm
    n = i * (nj - njq) + (j - njq)
    slot = n & 1

    def copies(dst_ref, sl, head0):
        return [pltpu.make_async_copy(stage_ref.at[sl, :, pl.ds(hh * SB_HEAD_DIM, SB_HEAD_DIM)],
                                      dst_ref.at[b, pl.ds(t0, tm), head0 + hh, :], sem_ref.at[sl])
                for hh in range(hpt)]

    def meta_copies(dst_ref, head0):
        return [pltpu.make_async_copy(mkv_ref.at[:, pl.ds(hh * SB_HEAD_DIM, SB_HEAD_DIM)],
                                      dst_ref.at[b, pl.ds(0, N_META), head0 + hh, :], msem_ref)
                for hh in range(hpt)]

    def send(dst_ref, head0):
        for c in copies(dst_ref, slot, head0):
            c.start()

        @pl.when(il == 0)
        def _():
            cs = meta_copies(dst_ref, head0)
            for c in cs:
                c.start()
            for c in cs:
                c.wait()

    @pl.when(j >= njq)
    def _():
        @pl.when(n >= 2)
        def _():
            for c in copies(kp_ref, slot, 0):
                c.wait()

        stage_ref[slot] = r

        @pl.when(j < 2 * njq)
        def _():
            send(kp_ref, (j - njq) * hpt)

        @pl.when(j >= 2 * njq)
        def _():
            send(vp_ref, (j - 2 * njq) * hpt)

        @pl.when(jnp.logical_and(i == last_i, j == nj - 1))
        def _():
            for c in copies(kp_ref, slot, 0) + copies(kp_ref, 1 - slot, 0):
                c.wait()


def _qkv_prompt(x, gamma, w, meta_kv, nb, seq):
    tm, tn = TM_PROMPT, 1024
    tpb = seq // tm
    rows = nb * seq
    kv_shape = jax.ShapeDtypeStruct((nb, N_META + seq, SB_HEADS, SB_HEAD_DIM), f32)
    return pl.pallas_call(
        functools.partial(_qkv_prompt_body, tm=tm, tn=tn, tpb=tpb),
        grid=(rows // tm, 3 * D_MODEL // tn),
        in_specs=[
            pl.BlockSpec((tm, D_MODEL), lambda i, j: (i, 0)),
            pl.BlockSpec((1, D_MODEL), lambda i, j: (0, 0)),
            pl.BlockSpec((D_MODEL, 3 * D_MODEL), lambda i, j: (0, 0), pipeline_mode=pl.Buffered(1)),
            pl.BlockSpec((N_META, tn), lambda i, j: (0, j)),
        ],
        out_specs=[
            pl.BlockSpec((tm, tn), lambda i, j: (i, j)),
            pl.BlockSpec(memory_space=pl.ANY),
            pl.BlockSpec(memory_space=pl.ANY),
        ],
        out_shape=[jax.ShapeDtypeStruct((rows, 3 * D_MODEL), bf16), kv_shape, kv_shape],
        scratch_shapes=[
            pltpu.VMEM((tm, D_MODEL), bf16),
            pltpu.VMEM((2, tm, tn), f32),
            pltpu.SemaphoreType.DMA((2,)),
            pltpu.SemaphoreType.DMA(()),
        ],
        compiler_params=_cparams(("arbitrary", "arbitrary")),
        name="qkv_prompt",
    )(x, gamma.reshape(1, D_MODEL), w, meta_kv)


def _proj_res_body(a_ref, w_ref, x_ref, o_ref):
    o_ref[...] = x_ref[...] + jnp.dot(a_ref[...], w_ref[...], preferred_element_type=f32)


def _proj_res(a, w, x, tm):
    rows = x.shape[0]
    return pl.pallas_call(
        _proj_res_body,
        grid=(rows // tm,),
        in_specs=[
            pl.BlockSpec((tm, D_MODEL), lambda i: (i, 0)),
            pl.BlockSpec((D_MODEL, D_MODEL), lambda i: (0, 0)),
            pl.BlockSpec((tm, D_MODEL), lambda i: (i, 0)),
        ],
        out_specs=pl.BlockSpec((tm, D_MODEL), lambda i: (i, 0)),
        out_shape=jax.ShapeDtypeStruct((rows, D_MODEL), f32),
        compiler_params=_cparams(("parallel",)),
        name="proj_res",
    )(a, w, x)


def _ssm_body(u_ref, h0r_ref, h0i_ref, m_ref, er_ref, ei_ref, fr_ref, fi_ref, lr_ref, li_ref,
              y_ref, hTr_ref, hTi_ref, dr_ref, di_ref, sr_ref, si_ref, *, ns, nch):
    nc = ns * nch
    lhs = jnp.concatenate([u_ref[pl.ds(s, nc, stride=CHUNK), :].astype(bf16) for s in range(CHUNK)], axis=1)
    dr_ref[...] = jnp.dot(lhs, er_ref[0], preferred_element_type=f32)
    di_ref[...] = jnp.dot(lhs, ei_ref[0], preferred_element_type=f32)
    lr = lr_ref[0]
    li = li_ref[0]
    h0r = h0r_ref[0, 0]
    h0i = h0i_ref[0, 0]
    if nch == 1:
        sr_ref[...] = h0r
        si_ref[...] = h0i
        hr = lr * h0r - li * h0i + dr_ref[...]
        hi = lr * h0i + li * h0r + di_ref[...]
    else:
        assert ns == 1

        def step(c, carry):
            hr, hi = carry
            sr_ref[pl.ds(c, 1), :] = hr
            si_ref[pl.ds(c, 1), :] = hi
            return (lr * hr - li * hi + dr_ref[pl.ds(c, 1), :], lr * hi + li * hr + di_ref[pl.ds(c, 1), :])

        hr, hi = lax.fori_loop(0, nch, step, (h0r, h0i), unroll=8)
    hTr_ref[0, 0] = hr
    hTi_ref[0, 0] = hi

    y = (jnp.dot(lhs, m_ref[0], preferred_element_type=f32)
         + jnp.dot(sr_ref[...].astype(bf16), fr_ref[0], preferred_element_type=f32)
         + jnp.dot(si_ref[...].astype(bf16), fi_ref[0], preferred_element_type=f32))
    for t in range(CHUNK):
        y_ref[pl.ds(t, nc, stride=CHUNK), :] = y[:, t * LANES:(t + 1) * LANES]


def _ssm_scan(u, h0r, h0i, mats, nb, ns, nch):
    m, er, ei, fr, fi, lr, li = mats
    rows_b = ns * nch * CHUNK
    nc = ns * nch
    w3 = lambda j, b: (j, 0, 0)
    h4 = lambda j, b: (j, 0, 0, 0)
    return pl.pallas_call(
        functools.partial(_ssm_body, ns=ns, nch=nch),
        grid=(SLICES, nb),
        in_specs=[
            pl.BlockSpec((rows_b, LANES), lambda j, b: (b, SLICES + j)),
            pl.BlockSpec((1, 1, ns, SLICE_STATE), h4),
            pl.BlockSpec((1, 1, ns, SLICE_STATE), h4),
            pl.BlockSpec((1, SLICE_IN, SLICE_IN), w3),
            pl.BlockSpec((1, SLICE_IN, SLICE_STATE), w3),
            pl.BlockSpec((1, SLICE_IN, SLICE_STATE), w3),
            pl.BlockSpec((1, SLICE_STATE, SLICE_IN), w3),
            pl.BlockSpec((1, SLICE_STATE, SLICE_IN), w3),
            pl.BlockSpec((1, 1, SLICE_STATE), w3),
            pl.BlockSpec((1, 1, SLICE_STATE), w3),
        ],
        out_specs=[
            pl.BlockSpec((rows_b, LANES), lambda j, b: (b, j)),
            pl.BlockSpec((1, 1, ns, SLICE_STATE), lambda j, b: (j, b, 0, 0)),
            pl.BlockSpec((1, 1, ns, SLICE_STATE), lambda j, b: (j, b, 0, 0)),
        ],
        out_shape=[
            jax.ShapeDtypeStruct((nb * rows_b, SSM_WIDTH), f32),
            jax.ShapeDtypeStruct((SLICES, nb, ns, SLICE_STATE), f32),
            jax.ShapeDtypeStruct((SLICES, nb, ns, SLICE_STATE), f32),
        ],
        scratch_shapes=[pltpu.VMEM((nc, SLICE_STATE), f32)] * 4,
        compiler_params=_cparams(("arbitrary", "arbitrary")),
        name="ssm_scan",
    )(u, h0r, h0i, m, er, ei, fr, fi, lr, li)


def _ssm_matrices(a_re, a_im, log_dt, b_re, b_im, c_re, c_im):
    hp = lax.Precision.HIGHEST
    N, P, T = SSM_STATE, SSM_GROUP, CHUNK
    a_re = a_re.astype(f32)
    a_im = a_im.astype(f32)
    dt = jnp.exp(log_dt.astype(f32))[:, None]
    mag = jnp.exp(a_re * dt)
    lb_re = mag * jnp.cos(a_im * dt)
    lb_im = mag * jnp.sin(a_im * dt)
    den = a_re * a_re + a_im * a_im
    k_re = ((lb_re - 1.0) * a_re + lb_im * a_im) / den
    k_im = (lb_im * a_re - (lb_re - 1.0) * a_im) / den
    b_re = b_re.astype(f32)
    b_im = b_im.astype(f32)
    bb_re = k_re[..., None] * b_re - k_im[..., None] * b_im
    bb_im = k_re[..., None] * b_im + k_im[..., None] * b_re
    cr = c_re.astype(f32)
    ci = c_im.astype(f32)

    def pw_step(c, _):
        re, im = c
        return (re * lb_re - im * lb_im, re * lb_im + im * lb_re), (re, im)

    _, (pw_re, pw_im) = lax.scan(pw_step, (jnp.ones_like(lb_re), jnp.zeros_like(lb_re)), None, length=T + 1)

    def block_diag(x, rg):
        rows, w = x.shape[-2:]
        rep = jnp.tile(jnp.eye(w, dtype=f32), (1, GPS))
        same = (jnp.arange(rows)[:, None] // rg) == (jnp.arange(GPS * w)[None, :] // w)
        return jnp.where(same, jnp.matmul(x, rep, precision=hp), 0.0)

    x_re = cr[:, :, :, None] * bb_re[:, None] - ci[:, :, :, None] * bb_im[:, None]
    x_im = cr[:, :, :, None] * bb_im[:, None] + ci[:, :, :, None] * bb_re[:, None]
    kern = (jnp.einsum('tgn,gqnp->tgqp', pw_re[:T], x_re, precision=hp)
            - jnp.einsum('tgn,gqnp->tgqp', pw_im[:T], x_im, precision=hp))
    kern = kern.reshape(T, SLICES, GPS, P, P).transpose(1, 0, 2, 4, 3).reshape(SLICES, T, LANES, P)
    m = _toeplitz_expand(block_diag(kern, P).astype(bf16))

    rev_re = pw_re[:T][::-1]
    rev_im = pw_im[:T][::-1]
    e_re = rev_re[:, :, :, None] * bb_re[None] - rev_im[:, :, :, None] * bb_im[None]
    e_im = rev_re[:, :, :, None] * bb_im[None] + rev_im[:, :, :, None] * bb_re[None]

    def slice_e(e):
        e = e.reshape(T, SLICES, GPS, N, P).transpose(1, 0, 2, 4, 3).reshape(SLICES, T, LANES, N)
        return block_diag(e, P).astype(bf16).reshape(SLICES, SLICE_IN, SLICE_STATE)

    nx_re = pw_re[1:]
    nx_im = pw_im[1:]
    crt = cr.transpose(0, 2, 1)
    cit = ci.transpose(0, 2, 1)
    f_re = crt[None] * nx_re[:, :, :, None] - cit[None] * nx_im[:, :, :, None]
    f_im = -(crt[None] * nx_im[:, :, :, None] + cit[None] * nx_re[:, :, :, None])

    def slice_f(f):
        f = f.reshape(T, SLICES, SLICE_STATE, P).transpose(1, 0, 2, 3)
        f = block_diag(f, N).astype(bf16)
        return f.transpose(0, 2, 1, 3).reshape(SLICES, SLICE_STATE, SLICE_IN)

    lam_re = pw_re[T].reshape(SLICES, 1, SLICE_STATE)
    lam_im = pw_im[T].reshape(SLICES, 1, SLICE_STATE)
    return m, slice_e(e_re), slice_e(e_im), slice_f(f_re), slice_f(f_im), lam_re, lam_im


def _toeplitz_body(bd_ref, o_ref):
    s = pl.program_id(1)
    for t in range(CHUNK):
        blk = bd_ref[0, jnp.maximum(t - s, 0)]
        o_ref[0, :, t * LANES:(t + 1) * LANES] = jnp.where(t >= s, blk, jnp.zeros_like(blk))


def _toeplitz_expand(bd):
    return pl.pallas_call(
        _toeplitz_body,
        grid=(SLICES, CHUNK),
        in_specs=[pl.BlockSpec((1, CHUNK, LANES, LANES), lambda j, s: (j, 0, 0, 0))],
        out_specs=pl.BlockSpec((1, LANES, SLICE_IN), lambda j, s: (j, s, 0)),
        out_shape=jax.ShapeDtypeStruct((SLICES, SLICE_IN, SLICE_IN), bf16),
        compiler_params=_cparams(("parallel", "arbitrary")),
        name="toeplitz_expand",
    )(bd)


def _state_to_slices(h, pad_streams):
    n = h.shape[0]
    t = h.astype(f32).reshape(n, SLICES, SLICE_STATE).transpose(1, 0, 2)
    return jnp.pad(t, ((0, 0), (0, pad_streams - n), (0, 0)))[:, None]


def _state_from_slices(h):
    return h.transpose(1, 0, 2).reshape(h.shape[1], SSM_GROUPS, SSM_STATE)


def _mix0_body(u_ref, y_ref, x_ref, hist_ref, p0_ref, pw_ref, ps_ref, d_ref, wglu_ref, bglu_ref, wout_ref,
               o_ref, ext_ref, *, sb, sl):
    rows = sb * sl
    ext_ref[:, :N_META, :] = hist_ref[...]
    ext_ref[:, N_META:, :] = u_ref[:, :, :POOL_WIDTH]
    step = lax.broadcasted_iota(jnp.int32, (sb, sl, POOL_GROUP), 1).astype(f32)
    seen = p0_ref[...] + step + 1.0

    acc = x_ref[...].reshape(rows, D_MODEL)
    for gi, w in enumerate(POOL_WINDOWS):
        cols = slice(gi * POOL_GROUP, (gi + 1) * POOL_GROUP)
        tot = ext_ref[:, N_META:N_META + sl, cols]
        for k in range(1, w):
            tot = tot + ext_ref[:, N_META - k:N_META - k + sl, cols]
        mean = tot / jnp.minimum(seen, float(w))
        diff = (mean - ext_ref[:, N_META:N_META + sl, cols]).reshape(rows, POOL_GROUP)
        ya = jnp.dot(diff.astype(bf16), pw_ref[gi], preferred_element_type=f32) * ps_ref[:, cols]
        acc = acc + jnp.dot(ya.astype(bf16), wout_ref[cols, :], preferred_element_type=f32)

    us = u_ref[:, :, POOL_WIDTH:].reshape(rows, SSM_WIDTH)
    y = y_ref[...].reshape(rows, SSM_WIDTH) + d_ref[...] * us
    z = jax.nn.gelu(y)
    gate = jax.nn.sigmoid(jnp.dot(z.astype(bf16), wglu_ref[...], preferred_element_type=f32) + bglu_ref[...])
    yb = z * gate
    acc = acc + jnp.dot(yb.astype(bf16), wout_ref[POOL_WIDTH:, :], preferred_element_type=f32)
    o_ref[...] = acc.reshape(sb, sl, D_MODEL)


def _mix0(u, y, x, hist, p0, pool_w, pool_scale, ssm_d, w_glu, b_glu, w_out, sb, sl):
    nseg = u.shape[0] // sl
    u3 = u.reshape(nseg, sl, D_MODEL)
    y3 = y.reshape(nseg, sl, SSM_WIDTH)
    x3 = x.reshape(nseg, sl, D_MODEL)
    seg3 = lambda i: (i, 0, 0)
    const2 = lambda i: (0, 0)
    out = pl.pallas_call(
        functools.partial(_mix0_body, sb=sb, sl=sl),
        grid=(nseg // sb,),
        in_specs=[
            pl.BlockSpec((sb, sl, D_MODEL), seg3),
            pl.BlockSpec((sb, sl, SSM_WIDTH), seg3),
            pl.BlockSpec((sb, sl, D_MODEL), seg3),
            pl.BlockSpec((sb, N_META, POOL_WIDTH), seg3),
            pl.BlockSpec((sb, 1, POOL_GROUP), seg3),
            pl.BlockSpec((len(POOL_WINDOWS), POOL_GROUP, POOL_GROUP), lambda i: (0, 0, 0)),
            pl.BlockSpec((1, POOL_WIDTH), const2),
            pl.BlockSpec((1, SSM_WIDTH), const2),
            pl.BlockSpec((SSM_WIDTH, SSM_WIDTH), const2),
            pl.BlockSpec((1, SSM_WIDTH), const2),
            pl.BlockSpec((D_MODEL, D_MODEL), const2),
        ],
        out_specs=pl.BlockSpec((sb, sl, D_MODEL), seg3),
        out_shape=jax.ShapeDtypeStruct((nseg, sl, D_MODEL), f32),
        scratch_shapes=[pltpu.VMEM((sb, N_META + sl, POOL_WIDTH), f32)],
        compiler_params=_cparams(("parallel",)),
        name="mix0",
    )(u3, y3, x3, hist, p0, pool_w, pool_scale.reshape(1, POOL_WIDTH), ssm_d.reshape(1, SSM_WIDTH),
      w_glu, b_glu.reshape(1, SSM_WIDTH), w_out)
    return out.reshape(nseg * sl, D_MODEL)


def _sb_tiles(qs, kts, vts, tri, carries, accs, mask, chained=False):
    ss = _scores(qs, kts)
    return _sb_from_scores(ss, vts, tri, carries, accs, mask, chained)[:2]


def _scores(qs, kts):
    return [lax.dot_general(q, kt, (((1,), (1,)), ((), ())), preferred_element_type=f32) for q, kt in zip(qs, kts)]


def _sb_from_scores(ss, vts, tri, carries, accs, mask, chained=False, next_scores=None):
    n = len(ss)
    log_betas, log_1mbs, his, los = [], [], [], []
    for s in ss:
        z = s * SB_SCALE_LOG2
        neg_abs = pltpu.bitcast(pltpu.bitcast(z, jnp.uint32) | jnp.uint32(0x80000000), f32)
        t = jnp.log(1.0 + jnp.exp2(neg_abs)) * LOG2E
        log_beta = jnp.minimum(z, 0.0) - t
        log_1mb = log_beta - z
        if mask is not None:
            log_1mb = jnp.where(mask, log_1mb, 0.0)
        hi = log_1mb.astype(bf16)
        log_betas.append(log_beta)
        log_1mbs.append(log_1mb)
        his.append(hi)
        los.append((log_1mb - hi.astype(f32)).astype(bf16))
    sums = [jnp.dot(his[i], tri, preferred_element_type=f32) + jnp.dot(los[i], tri, preferred_element_type=f32)
            for i in range(n)]
    if chained:
        carry = carries[0]
        carries = []
        for i in range(n):
            carries.append(carry)
            carry = carry + sums[i][:, :1] + log_1mbs[i][:, :1]
    nxt = None if next_scores is None else next_scores()
    afters = [sums[i] + carries[i] for i in range(n)]
    ws = []
    for i in range(n):
        w = jnp.exp2(log_betas[i] + afters[i])
        if mask is not None:
            w = jnp.where(mask, w, 0.0)
        ws.append(w.astype(bf16))
    if chained:
        acc = accs[0]
        for i in range(n):
            acc = acc + jnp.dot(ws[i], vts[i], preferred_element_type=f32)
        return [carry], [acc], nxt
    accs = [accs[i] + jnp.dot(ws[i], vts[i], preferred_element_type=f32) for i in range(n)]
    carries = [afters[i][:, :1] + log_1mbs[i][:, :1] for i in range(n)]
    return carries, accs, nxt


def _sb_tile(q, kt, vt, tri, carry, acc, mask):
    carries, accs = _sb_tiles([q], [kt], [vt], tri, [carry], [acc], mask)
    return carries[0], accs[0]


def _attn_prompt_body(q_ref, k_ref, v_ref, km_ref, vm_ref, tri_ref, o_ref, *, hb):
    tq = ATTN_TILE
    qi = pl.program_id(2)
    tri = tri_ref[...]
    cols = [slice(h * SB_HEAD_DIM, (h + 1) * SB_HEAD_DIM) for h in range(hb)]
    qs = [q_ref[:, c] for c in cols]

    def sweep(state, j, mask):
        r0 = pl.multiple_of(j * tq, tq)
        kts = [k_ref[pl.ds(r0, tq), c] for c in cols]
        vts = [v_ref[pl.ds(r0, tq), c] for c in cols]
        carries, accs = _sb_tiles(qs, kts, vts, tri, state[0], state[1], mask)
        return tuple(carries), tuple(accs)

    row = lax.broadcasted_iota(jnp.int32, (tq, tq), 0)
    col = lax.broadcasted_iota(jnp.int32, (tq, tq), 1)
    state = (tuple(jnp.zeros((tq, 1), f32) for _ in range(hb)),
             tuple(jnp.zeros((tq, SB_HEAD_DIM), f32) for _ in range(hb)))
    state = sweep(state, qi, col < row)
    state = lax.fori_loop(0, qi, lambda it, st: sweep(st, qi - 1 - it, None), state)
    colm = lax.broadcasted_iota(jnp.int32, (tq, 128), 1)
    _, accs = _sb_tiles(qs, [km_ref[:, c] for c in cols], [vm_ref[:, c] for c in cols], tri_ref[:128, :128],
                        state[0], state[1], colm < N_META)
    for h in range(hb):
        o_ref[:, cols[h]] = accs[h].astype(o_ref.dtype)


def _attn_prompt(qkvb, kmeta, vmeta, tri, nb, seq):
    tq = ATTN_TILE
    nq = seq // tq
    hb = ATTN_HEADS_PER_STEP
    wd = hb * SB_HEAD_DIM
    ng = SB_HEADS // hb
    return pl.pallas_call(
        functools.partial(_attn_prompt_body, hb=hb),
        grid=(nb, ng, nq),
        in_specs=[
            pl.BlockSpec((tq, wd), lambda b, g, i: (b * nq + i, g)),
            pl.BlockSpec((seq, wd), lambda b, g, i: (b, ng + g)),
            pl.BlockSpec((seq, wd), lambda b, g, i: (b, 2 * ng + g)),
            pl.BlockSpec((128, wd), lambda b, g, i: (0, g)),
            pl.BlockSpec((128, wd), lambda b, g, i: (0, g)),
            pl.BlockSpec((tq, tq), lambda b, g, i: (0, 0)),
        ],
        out_specs=pl.BlockSpec((tq, wd), lambda b, g, i: (b * nq + i, g)),
        out_shape=jax.ShapeDtypeStruct((nb * seq, D_MODEL), bf16),
        compiler_params=_cparams(("parallel", "parallel", "arbitrary"), ATTN_FLAGS),
        name="attn_prompt",
    )(qkvb, qkvb, qkvb, kmeta, vmeta, tri)


def _stack_heads(q):
    ln = q.shape[0]
    rows = SB_HEADS * ln
    qt = jnp.concatenate([q] * SB_HEADS, axis=0)
    rh = lax.shift_right_logical(lax.broadcasted_iota(jnp.int32, (rows, D_MODEL), 0), _log2(ln))
    ch = lax.shift_right_logical(lax.broadcasted_iota(jnp.int32, (rows, D_MODEL), 1), _log2(SB_HEAD_DIM))
    return jnp.where(rh == ch, qt, jnp.zeros_like(qt))


def _own_tile(q_ref, ko_ref, vo_ref, tri_ref, ln):
    rows = SB_HEADS * ln
    qbd = _stack_heads(q_ref[...])
    row = lax.broadcasted_iota(jnp.int32, (rows, 128), 0)
    col = lax.broadcasted_iota(jnp.int32, (rows, 128), 1)
    mask = col < (row & (ln - 1))
    carry = jnp.zeros((rows, 1), f32)
    acc = jnp.zeros((rows, D_MODEL), f32)
    carry, acc = _sb_tile(qbd, ko_ref[...], vo_ref[...], tri_ref[:128, :128], carry, acc, mask)
    return qbd, carry, acc


def _write_heads(o_ref, acc, ln):
    for h in range(SB_HEADS):
        cols = slice(h * SB_HEAD_DIM, (h + 1) * SB_HEAD_DIM)
        o_ref[:, cols] = acc[h * ln:(h + 1) * ln, cols].astype(o_ref.dtype)


def _head_major_tile(c_ref, sub):
    base = sub * ATTN_TILE * SB_HEADS
    return jnp.concatenate(
        [c_ref[pl.ds(base + h, ATTN_TILE, stride=SB_HEADS), :] for h in range(SB_HEADS)], axis=1).astype(bf16)


def _attn_cached_body(q_ref, ko_ref, vo_ref, kc_ref, vc_ref, tri_ref, o_ref, qbd_ref, carry_ref, acc_ref, *, ln, tk):
    kt = pl.program_id(1)

    @pl.when(kt == 0)
    def _():
        qbd, carry, acc = _own_tile(q_ref, ko_ref, vo_ref, tri_ref, ln)
        qbd_ref[...] = qbd
        carry_ref[...] = carry
        acc_ref[...] = acc

    qbd = qbd_ref[...]
    subs = list(reversed(range(tk // ATTN_TILE)))
    carries, accs = _sb_tiles([qbd] * len(subs), [_head_major_tile(kc_ref, s) for s in subs],
                              [_head_major_tile(vc_ref, s) for s in subs], tri_ref[...],
                              [carry_ref[...]], [acc_ref[...]], None, chained=True)
    carry = carries[0]
    acc = accs[0]
    carry_ref[...] = carry
    acc_ref[...] = acc

    @pl.when(kt == pl.num_programs(1) - 1)
    def _():
        _write_heads(o_ref, acc, ln)


def _attn_cached(qkvb, k_own, v_own, k_cache, v_cache, tri, nb, ln, tk):
    past = k_cache.shape[0] // (nb * SB_HEADS)
    nkt = past // tk
    rows = SB_HEADS * ln
    return pl.pallas_call(
        functools.partial(_attn_cached_body, ln=ln, tk=tk),
        grid=(nb, nkt),
        in_specs=[
            pl.BlockSpec((ln, D_MODEL), lambda b, t: (b, 0)),
            pl.BlockSpec((128, D_MODEL), lambda b, t: (b, 0)),
            pl.BlockSpec((128, D_MODEL), lambda b, t: (b, 0)),
            pl.BlockSpec((tk * SB_HEADS, SB_HEAD_DIM), lambda b, t: (b * nkt + nkt - 1 - t, 0)),
            pl.BlockSpec((tk * SB_HEADS, SB_HEAD_DIM), lambda b, t: (b * nkt + nkt - 1 - t, 0)),
            pl.BlockSpec((ATTN_TILE, ATTN_TILE), lambda b, t: (0, 0)),
        ],
        out_specs=pl.BlockSpec((ln, D_MODEL), lambda b, t: (b, 0)),
        out_shape=jax.ShapeDtypeStruct((nb * ln, D_MODEL), bf16),
        scratch_shapes=[
            pltpu.VMEM((rows, D_MODEL), bf16),
            pltpu.VMEM((rows, 1), f32),
            pltpu.VMEM((rows, D_MODEL), f32),
        ],
        compiler_params=_cparams(("parallel", "arbitrary")),
        name="attn_cached",
    )(qkvb, k_own, v_own, k_cache, v_cache, tri)


def _attn_own_body(q_ref, ko_ref, vo_ref, tri_ref, o_ref, *, ln):
    _, _, acc = _own_tile(q_ref, ko_ref, vo_ref, tri_ref, ln)
    _write_heads(o_ref, acc, ln)


def _attn_own(q, k_own, v_own, tri, ln):
    full = lambda i: (0, 0)
    return pl.pallas_call(
        functools.partial(_attn_own_body, ln=ln),
        grid=(1,),
        in_specs=[
            pl.BlockSpec((ln, D_MODEL), full),
            pl.BlockSpec((128, D_MODEL), full),
            pl.BlockSpec((128, D_MODEL), full),
            pl.BlockSpec((ATTN_TILE, ATTN_TILE), full),
        ],
        out_specs=pl.BlockSpec((ln, D_MODEL), full),
        out_shape=jax.ShapeDtypeStruct((ln, D_MODEL), bf16),
        compiler_params=_cparams(("arbitrary",)),
        name="attn_own",
    )(q, k_own, v_own, tri)


def kernel(x_prompt, x_sample, cache_pool, state_ssm_re, state_ssm_im, cache_k, cache_v, meta_tokens, ffn_norm, ffn_w_gate, ffn_w_up, ffn_w_down, mix_norm, ab_w_in, pool_w, pool_scale, ssm_a_re, ssm_a_im, ssm_log_dt, ssm_b_re, ssm_b_im, ssm_c_re, ssm_c_im, ssm_d, ssm_w_glu, ssm_b_glu, ab_w_out, sb_w_qkv, sb_w_out, final_norm):
    nb, seq, _ = x_prompt.shape
    db, dl, _ = x_sample.shape
    assert dl == N_META == CHUNK and seq % ATTN_TILE == 0
    n_small = db + 1
    rows_s = n_small * dl
    tm_p = TM_PROMPT

    wg = ffn_w_gate.astype(bf16)
    wu = ffn_w_up.astype(bf16)
    wd = ffn_w_down.astype(bf16)

    def ffn(x, which, tm, final_gamma=None):
        return _ffn(x, ffn_norm[which], wg, wu, wd, which, tm, final_gamma)

    w_in = ab_w_in[0].astype(bf16)
    w_out0 = ab_w_out[0].astype(bf16)
    w_pool = pool_w[0].astype(bf16)
    w_glu = ssm_w_glu[0].astype(bf16)
    w_qkv = sb_w_qkv[0].astype(bf16)
    w_out1 = sb_w_out[0].astype(bf16)
    mats = _ssm_matrices(ssm_a_re[0], ssm_a_im[0], ssm_log_dt[0], ssm_b_re[0], ssm_b_im[0],
                         ssm_c_re[0], ssm_c_im[0])
    ti = jnp.arange(ATTN_TILE)
    tri = (ti[:, None] > ti[None, :]).astype(bf16)

    xp = x_prompt.reshape(nb * seq, D_MODEL)
    xs = jnp.concatenate([x_sample.reshape(db * dl, D_MODEL), meta_tokens.astype(f32)], axis=0)

    xp = ffn(xp, (0, 0), TM_FFN)
    xs = ffn(xs, (0, 0), rows_s)
    up = _norm_proj(xp, mix_norm[0], w_in, tm_p, D_MODEL, False)
    us = _norm_proj(xs, mix_norm[0], w_in, rows_s, D_MODEL, False)

    pad_s = -(-n_small // 8) * 8
    zero_state = jnp.zeros((1, SSM_GROUPS, SSM_STATE), f32)
    h0r = _state_to_slices(jnp.concatenate([state_ssm_re[0], zero_state], axis=0), pad_s)
    h0i = _state_to_slices(jnp.concatenate([state_ssm_im[0], zero_state], axis=0), pad_s)
    us_pad = jnp.pad(us, ((0, (pad_s - n_small) * dl), (0, 0)))
    ys, hsr, hsi = _ssm_scan(us_pad, h0r, h0i, mats, 1, pad_s, 1)
    ys = ys[:rows_s]
    hsr = hsr[:, 0]
    hsi = hsi[:, 0]
    us3 = us.reshape(n_small, dl, D_MODEL)
    hist_s = jnp.concatenate([
        jnp.concatenate([jnp.zeros((db, 1, POOL_WIDTH), f32), cache_pool[0].astype(f32)], axis=1),
        jnp.zeros((1, N_META, POOL_WIDTH), f32)], axis=0)
    p0_s = jnp.concatenate([jnp.full((db, 1, POOL_GROUP), float(N_META), f32),
                            jnp.zeros((1, 1, POOL_GROUP), f32)], axis=0)
    xs = _mix0(us, ys, xs, hist_s, p0_s, w_pool, pool_scale[0], ssm_d[0], w_glu, ssm_b_glu[0], w_out0,
               n_small, dl)

    meta_r = hsr[:, db].reshape(SLICES, 1, 1, SLICE_STATE)
    meta_i = hsi[:, db].reshape(SLICES, 1, 1, SLICE_STATE)
    yp, hpr, hpi = _ssm_scan(up, meta_r, meta_i, mats, nb, 1, seq // CHUNK)
    sl_p = MIX0_SEG
    up4 = up.reshape(nb, seq // sl_p, sl_p, D_MODEL)
    meta_tail = jnp.broadcast_to(us3[db:, :, :POOL_WIDTH], (nb, N_META, POOL_WIDTH))
    hist_p = jnp.concatenate([meta_tail[:, None], up4[:, :-1, sl_p - N_META:, :POOL_WIDTH]], axis=1)
    hist_p = hist_p.reshape(nb * (seq // sl_p), N_META, POOL_WIDTH)
    p0_p = jnp.full((nb * (seq // sl_p), 1, POOL_GROUP), float(N_META), f32)
    xp = _mix0(up, yp, xp, hist_p, p0_p, w_pool, pool_scale[0], ssm_d[0], w_glu, ssm_b_glu[0], w_out0, 1, sl_p)

    xp = ffn(xp, (0, 1), TM_FFN)
    xs = ffn(xs, (0, 1), rows_s)

    xp = ffn(xp, (1, 0), TM_FFN)
    xs = ffn(xs, (1, 0), rows_s)
    qkv_s, qkvb_s = _norm_proj(xs, mix_norm[1], w_qkv, rows_s, 1024, True)
    qkvb_p, kp, vp = _qkv_prompt(xp, mix_norm[1], w_qkv, qkv_s[db * dl:], nb, seq)

    kcol = slice(D_MODEL, 2 * D_MODEL)
    vcol = slice(2 * D_MODEL, 3 * D_MODEL)
    pad_keys = lambda a: jnp.pad(a, ((0, 0), (0, 128 - dl), (0, 0))).reshape(-1, D_MODEL)
    kb3 = qkvb_s[:, kcol].reshape(n_small, dl, D_MODEL)
    vb3 = qkvb_s[:, vcol].reshape(n_small, dl, D_MODEL)
    k_own = pad_keys(kb3[:db])
    v_own = pad_keys(vb3[:db])
    k_meta = pad_keys(kb3[db:])
    v_meta = pad_keys(vb3[db:])

    past = cache_k.shape[2]
    o_sample = _attn_cached(qkvb_s, k_own, v_own, cache_k.reshape(db * past * SB_HEADS, SB_HEAD_DIM),
                            cache_v.reshape(db * past * SB_HEADS, SB_HEAD_DIM), tri, db, dl, CACHE_TILE)
    o_meta = _attn_own(qkvb_s[db * dl:, :D_MODEL], k_meta, v_meta, tri, dl)
    o_prompt = _attn_prompt(qkvb_p, k_meta, v_meta, tri, nb, seq)
    xp = _proj_res(o_prompt, w_out1, xp, tm_p)
    xs = _proj_res(jnp.concatenate([o_sample, o_meta], axis=0), w_out1, xs, rows_s)

    yp_out = ffn(xp, (1, 1), TM_FFN, final_norm)
    ys_out = ffn(xs, (1, 1), rows_s, final_norm)

    y_prompt = yp_out.reshape(nb, seq, D_MODEL)
    y_sample = ys_out[:db * dl].reshape(db, dl, D_MODEL)
    up3 = up.reshape(nb, seq, D_MODEL)
    pool_p = up3[:, seq - POOL_HIST:, :POOL_WIDTH][None]
    pool_s = us3[:db, dl - POOL_HIST:, :POOL_WIDTH][None]
    re_p = _state_from_slices(hpr[:, :, 0])[None]
    im_p = _state_from_slices(hpi[:, :, 0])[None]
    re_s = _state_from_slices(hsr[:, :db])[None]
    im_s = _state_from_slices(hsi[:, :db])[None]
    heads = lambda a, n: a.reshape(n, -1, SB_HEADS, SB_HEAD_DIM)
    k3 = qkv_s[:, kcol].reshape(n_small, dl, D_MODEL)
    v3 = qkv_s[:, vcol].reshape(n_small, dl, D_MODEL)
    k_p = kp[None]
    v_p = vp[None]
    k_s = heads(k3[:db], db)[None]
    v_s = heads(v3[:db], db)[None]
    return (y_prompt, y_sample, pool_p, pool_s, re_p, im_p, re_s, im_s, k_p, v_p, k_s, v_s)
```

```python
import functools
import math

import jax
import jax.numpy as jnp
from jax import lax
from jax.experimental import pallas as pl
from jax.experimental.pallas import tpu as pltpu

f32 = jnp.float32
bf16 = jnp.bfloat16

D_MODEL = 2048
N_META = 16
RMS_EPS = 1e-6
FFN_RESIDUAL = 0.5
FFN_DIM = 5632
POOL_WIDTH = 1024
POOL_WINDOWS = (2, 4, 8, 16)
POOL_GROUP = 256
POOL_HIST = 15
SSM_WIDTH = 1024
SSM_GROUP = 16
SSM_GROUPS = 64
SSM_STATE = 64
SB_HEADS = 16
SB_HEAD_DIM = 128
LOG2E = 1.4426950408889634
SB_SCALE_LOG2 = LOG2E / math.sqrt(SB_HEAD_DIM)

LANES = 128
CHUNK = 16
SLICES = SSM_WIDTH // LANES
GPS = LANES // SSM_GROUP
SLICE_IN = CHUNK * LANES
SLICE_STATE = GPS * SSM_STATE

VMEM_LIMIT = 56 * 1024 * 1024
ATTN_TILE = 256
ATTN_HEADS_PER_STEP = 8
ATTN_FLAGS = None
FFN_TILE_F = 512
FFN_TILE_F_SMALL = 256
TM_PROMPT = 512
TM_FFN = 1024
MIX0_SEG = 256
CACHE_TILE = 1024


def _cparams(sem, flags=None):
    return pltpu.CompilerParams(dimension_semantics=sem, vmem_limit_bytes=VMEM_LIMIT, flags=flags)


def _log2(n):
    assert n & (n - 1) == 0
    return n.bit_length() - 1


def _rms(xf, g):
    ms = jnp.mean(xf * xf, axis=-1, keepdims=True)
    return xf * lax.rsqrt(ms + RMS_EPS) * g


def _ffn_body(*refs, final, cast):
    refs = list(refs)
    x_ref, g_ref, wg_ref, wu_ref, wd_ref = refs[:5]
    gf_ref = refs[5] if final else None
    o_ref = refs[5 + final]
    h_ref = refs[-1]
    j = pl.program_id(1)

    @pl.when(j == 0)
    def _():
        xf = x_ref[...]
        h_ref[...] = _rms(xf, g_ref[...]).astype(bf16)
        o_ref[...] = xf

    wg, wu, wd = wg_ref[...], wu_ref[...], wd_ref[...]
    if cast:
        wgb_ref, wub_ref, wdb_ref = refs[6 + final:9 + final]
        wg, wu, wd = wg.astype(bf16), wu.astype(bf16), wd.astype(bf16)
        wgb_ref[...] = wg
        wub_ref[...] = wu
        wdb_ref[...] = wd

    h = h_ref[...]
    g = jnp.dot(h, wg, preferred_element_type=f32)
    u = jnp.dot(h, wu, preferred_element_type=f32)
    a = (jax.nn.silu(g) * u * FFN_RESIDUAL).astype(bf16)
    o_ref[...] += jnp.dot(a, wd, preferred_element_type=f32)

    if final:
        @pl.when(j == pl.num_programs(1) - 1)
        def _():
            o_ref[...] = _rms(o_ref[...], gf_ref[...])


def _ffn(x, gamma, wg, wu, wd, tm, tf, final_gamma=None, which=None):
    rows = x.shape[0]
    final = final_gamma is not None
    cast = which is not None
    if cast:
        assert rows == tm
        la, po = which
        w_specs = [
            pl.BlockSpec((None, None, D_MODEL, tf), lambda i, j: (la, po, 0, j)),
            pl.BlockSpec((None, None, D_MODEL, tf), lambda i, j: (la, po, 0, j)),
            pl.BlockSpec((None, None, tf, D_MODEL), lambda i, j: (la, po, j, 0)),
        ]
    else:
        w_specs = [
            pl.BlockSpec((D_MODEL, tf), lambda i, j: (0, j)),
            pl.BlockSpec((D_MODEL, tf), lambda i, j: (0, j)),
            pl.BlockSpec((tf, D_MODEL), lambda i, j: (j, 0)),
        ]
    in_specs = [pl.BlockSpec((tm, D_MODEL), lambda i, j: (i, 0)), pl.BlockSpec((1, D_MODEL), lambda i, j: (0, 0))]
    in_specs += w_specs
    args = [x, gamma.reshape(1, D_MODEL), wg, wu, wd]
    if final:
        in_specs.append(pl.BlockSpec((1, D_MODEL), lambda i, j: (0, 0)))
        args.append(final_gamma.reshape(1, D_MODEL))
    out_specs = [pl.BlockSpec((tm, D_MODEL), lambda i, j: (i, 0))]
    out_shape = [jax.ShapeDtypeStruct((rows, D_MODEL), f32)]
    if cast:
        out_specs += [
            pl.BlockSpec((D_MODEL, tf), lambda i, j: (0, j)),
            pl.BlockSpec((D_MODEL, tf), lambda i, j: (0, j)),
            pl.BlockSpec((tf, D_MODEL), lambda i, j: (j, 0)),
        ]
        out_shape += [
            jax.ShapeDtypeStruct((D_MODEL, FFN_DIM), bf16),
            jax.ShapeDtypeStruct((D_MODEL, FFN_DIM), bf16),
            jax.ShapeDtypeStruct((FFN_DIM, D_MODEL), bf16),
        ]
    res = pl.pallas_call(
        functools.partial(_ffn_body, final=final, cast=cast),
        grid=(rows // tm, FFN_DIM // tf),
        in_specs=in_specs,
        out_specs=out_specs,
        out_shape=out_shape,
        scratch_shapes=[pltpu.VMEM((tm, D_MODEL), bf16)],
        compiler_params=_cparams(("arbitrary", "arbitrary")),
        name=("ffn_final" if final else "ffn") + ("_cast" if cast else ""),
    )(*args)
    return res if cast else res[0]


def _norm_proj_body(x_ref, g_ref, w_ref, *refs, with_bf16):
    if with_bf16:
        o_ref, ob_ref, h_ref = refs
    else:
        o_ref, h_ref = refs

    @pl.when(pl.program_id(1) == 0)
    def _():
        h_ref[...] = _rms(x_ref[...], g_ref[...]).astype(bf16)

    r = jnp.dot(h_ref[...], w_ref[...], preferred_element_type=f32)
    o_ref[...] = r
    if with_bf16:
        ob_ref[...] = r.astype(bf16)


def _norm_proj(x, gamma, w, tm, tn, with_bf16):
    rows = x.shape[0]
    n = w.shape[1]
    out_shape = [jax.ShapeDtypeStruct((rows, n), f32)]
    out_specs = [pl.BlockSpec((tm, tn), lambda i, j: (i, j))]
    if with_bf16:
        out_shape.append(jax.ShapeDtypeStruct((rows, n), bf16))
        out_specs.append(pl.BlockSpec((tm, tn), lambda i, j: (i, j)))
    res = pl.pallas_call(
        functools.partial(_norm_proj_body, with_bf16=with_bf16),
        grid=(rows // tm, n // tn),
        in_specs=[
            pl.BlockSpec((tm, D_MODEL), lambda i, j: (i, 0)),
            pl.BlockSpec((1, D_MODEL), lambda i, j: (0, 0)),
            pl.BlockSpec((D_MODEL, tn), lambda i, j: (0, j)),
        ],
        out_specs=out_specs,
        out_shape=out_shape,
        scratch_shapes=[pltpu.VMEM((tm, D_MODEL), bf16)],
        compiler_params=_cparams(("parallel", "arbitrary")),
        name="norm_proj_qkv" if with_bf16 else "norm_proj",
    )(x, gamma.reshape(1, D_MODEL), w)
    return res if with_bf16 else res[0]


def _qkv_prompt_body(x_ref, g_ref, w_ref, mkv_ref, qkvb_ref, kp_ref, vp_ref, h_ref, stage_ref, sem_ref, msem_ref,
                     *, tm, tn, tpb):
    i = pl.program_id(0)
    j = pl.program_id(1)
    last_i = pl.num_programs(0) - 1
    nj = 3 * D_MODEL // tn
    njq = D_MODEL // tn
    hpt = tn // SB_HEAD_DIM

    @pl.when(j == 0)
    def _():
        h_ref[...] = _rms(x_ref[...], g_ref[...]).astype(bf16)

    r = jnp.dot(h_ref[...], w_ref[:, pl.ds(pl.multiple_of(j * tn, tn), tn)], preferred_element_type=f32)
    qkvb_ref[...] = r.astype(bf16)

    b = lax.shift_right_logical(i, _log2(tpb))
    il = i & (tpb - 1)
    t0 = N_META + il * tm
    n = i * (nj - njq) + (j - njq)
    slot = n & 1

    def copies(dst_ref, sl, head0):
        return [pltpu.make_async_copy(stage_ref.at[sl, :, pl.ds(hh * SB_HEAD_DIM, SB_HEAD_DIM)],
                                      dst_ref.at[b, pl.ds(t0, tm), head0 + hh, :], sem_ref.at[sl])
                for hh in range(hpt)]

    def meta_copies(dst_ref, head0):
        return [pltpu.make_async_copy(mkv_ref.at[:, pl.ds(hh * SB_HEAD_DIM, SB_HEAD_DIM)],
                                      dst_ref.at[b, pl.ds(0, N_META), head0 + hh, :], msem_ref)
                for hh in range(hpt)]

    def send(dst_ref, head0):
        for c in copies(dst_ref, slot, head0):
            c.start()

        @pl.when(il == 0)
        def _():
            cs = meta_copies(dst_ref, head0)
            for c in cs:
                c.start()
            for c in cs:
                c.wait()

    @pl.when(j >= njq)
    def _():
        @pl.when(n >= 2)
        def _():
            for c in copies(kp_ref, slot, 0):
                c.wait()

        stage_ref[slot] = r

        @pl.when(j < 2 * njq)
        def _():
            send(kp_ref, (j - njq) * hpt)

        @pl.when(j >= 2 * njq)
        def _():
            send(vp_ref, (j - 2 * njq) * hpt)

        @pl.when(jnp.logical_and(i == last_i, j == nj - 1))
        def _():
            for c in copies(kp_ref, slot, 0) + copies(kp_ref, 1 - slot, 0):
                c.wait()


def _qkv_prompt(x, gamma, w, meta_kv, nb, seq):
    tm, tn = TM_PROMPT, 1024
    tpb = seq // tm
    rows = nb * seq
    kv_shape = jax.ShapeDtypeStruct((nb, N_META + seq, SB_HEADS, SB_HEAD_DIM), f32)
    return pl.pallas_call(
        functools.partial(_qkv_prompt_body, tm=tm, tn=tn, tpb=tpb),
        grid=(rows // tm, 3 * D_MODEL // tn),
        in_specs=[
            pl.BlockSpec((tm, D_MODEL), lambda i, j: (i, 0)),
            pl.BlockSpec((1, D_MODEL), lambda i, j: (0, 0)),
            pl.BlockSpec((D_MODEL, 3 * D_MODEL), lambda i, j: (0, 0), pipeline_mode=pl.Buffered(1)),
            pl.BlockSpec((N_META, tn), lambda i, j: (0, j)),
        ],
        out_specs=[
            pl.BlockSpec((tm, tn), lambda i, j: (i, j)),
            pl.BlockSpec(memory_space=pl.ANY),
            pl.BlockSpec(memory_space=pl.ANY),
        ],
        out_shape=[jax.ShapeDtypeStruct((rows, 3 * D_MODEL), bf16), kv_shape, kv_shape],
        scratch_shapes=[
            pltpu.VMEM((tm, D_MODEL), bf16),
            pltpu.VMEM((2, tm, tn), f32),
            pltpu.SemaphoreType.DMA((2,)),
            pltpu.SemaphoreType.DMA(()),
        ],
        compiler_params=_cparams(("arbitrary", "arbitrary")),
        name="qkv_prompt",
    )(x, gamma.reshape(1, D_MODEL), w, meta_kv)


def _proj_res_body(a_ref, w_ref, x_ref, o_ref):
    o_ref[...] = x_ref[...] + jnp.dot(a_ref[...], w_ref[...], preferred_element_type=f32)


def _proj_res(a, w, x, tm):
    rows = x.shape[0]
    return pl.pallas_call(
        _proj_res_body,
        grid=(rows // tm,),
        in_specs=[
            pl.BlockSpec((tm, D_MODEL), lambda i: (i, 0)),
            pl.BlockSpec((D_MODEL, D_MODEL), lambda i: (0, 0)),
            pl.BlockSpec((tm, D_MODEL), lambda i: (i, 0)),
        ],
        out_specs=pl.BlockSpec((tm, D_MODEL), lambda i: (i, 0)),
        out_shape=jax.ShapeDtypeStruct((rows, D_MODEL), f32),
        compiler_params=_cparams(("parallel",)),
        name="proj_res",
    )(a, w, x)


def _ssm_body(u_ref, h0r_ref, h0i_ref, m_ref, er_ref, ei_ref, fr_ref, fi_ref, lr_ref, li_ref,
              y_ref, hTr_ref, hTi_ref, dr_ref, di_ref, sr_ref, si_ref, *, ns, nch):
    nc = ns * nch
    lhs = jnp.concatenate([u_ref[pl.ds(s, nc, stride=CHUNK), :].astype(bf16) for s in range(CHUNK)], axis=1)
    dr_ref[...] = jnp.dot(lhs, er_ref[0], preferred_element_type=f32)
    di_ref[...] = jnp.dot(lhs, ei_ref[0], preferred_element_type=f32)
    lr = lr_ref[0]
    li = li_ref[0]
    h0r = h0r_ref[0, 0]
    h0i = h0i_ref[0, 0]
    if nch == 1:
        sr_ref[...] = h0r
        si_ref[...] = h0i
        hr = lr * h0r - li * h0i + dr_ref[...]
        hi = lr * h0i + li * h0r + di_ref[...]
    else:
        assert ns == 1

        def step(c, carry):
            hr, hi = carry
            sr_ref[pl.ds(c, 1), :] = hr
            si_ref[pl.ds(c, 1), :] = hi
            return (lr * hr - li * hi + dr_ref[pl.ds(c, 1), :], lr * hi + li * hr + di_ref[pl.ds(c, 1), :])

        hr, hi = lax.fori_loop(0, nch, step, (h0r, h0i), unroll=8)
    hTr_ref[0, 0] = hr
    hTi_ref[0, 0] = hi

    y = (jnp.dot(lhs, m_ref[0], preferred_element_type=f32)
         + jnp.dot(sr_ref[...].astype(bf16), fr_ref[0], preferred_element_type=f32)
         + jnp.dot(si_ref[...].astype(bf16), fi_ref[0], preferred_element_type=f32))
    for t in range(CHUNK):
        y_ref[pl.ds(t, nc, stride=CHUNK), :] = y[:, t * LANES:(t + 1) * LANES]


def _ssm_scan(u, h0r, h0i, mats, nb, ns, nch):
    m, er, ei, fr, fi, lr, li = mats
    rows_b = ns * nch * CHUNK
    nc = ns * nch
    w3 = lambda j, b: (j, 0, 0)
    h4 = lambda j, b: (j, 0, 0, 0)
    return pl.pallas_call(
        functools.partial(_ssm_body, ns=ns, nch=nch),
        grid=(SLICES, nb),
        in_specs=[
            pl.BlockSpec((rows_b, LANES), lambda j, b: (b, SLICES + j)),
            pl.BlockSpec((1, 1, ns, SLICE_STATE), h4),
            pl.BlockSpec((1, 1, ns, SLICE_STATE), h4),
            pl.BlockSpec((1, SLICE_IN, SLICE_IN), w3),
            pl.BlockSpec((1, SLICE_IN, SLICE_STATE), w3),
            pl.BlockSpec((1, SLICE_IN, SLICE_STATE), w3),
            pl.BlockSpec((1, SLICE_STATE, SLICE_IN), w3),
            pl.BlockSpec((1, SLICE_STATE, SLICE_IN), w3),
            pl.BlockSpec((1, 1, SLICE_STATE), w3),
            pl.BlockSpec((1, 1, SLICE_STATE), w3),
        ],
        out_specs=[
            pl.BlockSpec((rows_b, LANES), lambda j, b: (b, j)),
            pl.BlockSpec((1, 1, ns, SLICE_STATE), lambda j, b: (j, b, 0, 0)),
            pl.BlockSpec((1, 1, ns, SLICE_STATE), lambda j, b: (j, b, 0, 0)),
        ],
        out_shape=[
            jax.ShapeDtypeStruct((nb * rows_b, SSM_WIDTH), f32),
            jax.ShapeDtypeStruct((SLICES, nb, ns, SLICE_STATE), f32),
            jax.ShapeDtypeStruct((SLICES, nb, ns, SLICE_STATE), f32),
        ],
        scratch_shapes=[pltpu.VMEM((nc, SLICE_STATE), f32)] * 4,
        compiler_params=_cparams(("arbitrary", "arbitrary")),
        name="ssm_scan",
    )(u, h0r, h0i, m, er, ei, fr, fi, lr, li)


def _ssm_matrices(a_re, a_im, log_dt, b_re, b_im, c_re, c_im):
    hp = lax.Precision.HIGHEST
    N, P, T = SSM_STATE, SSM_GROUP, CHUNK
    a_re = a_re.astype(f32)
    a_im = a_im.astype(f32)
    dt = jnp.exp(log_dt.astype(f32))[:, None]
    mag = jnp.exp(a_re * dt)
    lb_re = mag * jnp.cos(a_im * dt)
    lb_im = mag * jnp.sin(a_im * dt)
    den = a_re * a_re + a_im * a_im
    k_re = ((lb_re - 1.0) * a_re + lb_im * a_im) / den
    k_im = (lb_im * a_re - (lb_re - 1.0) * a_im) / den
    b_re = b_re.astype(f32)
    b_im = b_im.astype(f32)
    bb_re = k_re[..., None] * b_re - k_im[..., None] * b_im
    bb_im = k_re[..., None] * b_im + k_im[..., None] * b_re
    cr = c_re.astype(f32)
    ci = c_im.astype(f32)

    def pw_step(c, _):
        re, im = c
        return (re * lb_re - im * lb_im, re * lb_im + im * lb_re), (re, im)

    _, (pw_re, pw_im) = lax.scan(pw_step, (jnp.ones_like(lb_re), jnp.zeros_like(lb_re)), None, length=T + 1)

    def block_diag(x, rg):
        rows, w = x.shape[-2:]
        rep = jnp.tile(jnp.eye(w, dtype=f32), (1, GPS))
        same = (jnp.arange(rows)[:, None] // rg) == (jnp.arange(GPS * w)[None, :] // w)
        return jnp.where(same, jnp.matmul(x, rep, precision=hp), 0.0)

    x_re = cr[:, :, :, None] * bb_re[:, None] - ci[:, :, :, None] * bb_im[:, None]
    x_im = cr[:, :, :, None] * bb_im[:, None] + ci[:, :, :, None] * bb_re[:, None]
    kern = (jnp.einsum('tgn,gqnp->tgqp', pw_re[:T], x_re, precision=hp)
            - jnp.einsum('tgn,gqnp->tgqp', pw_im[:T], x_im, precision=hp))
    kern = kern.reshape(T, SLICES, GPS, P, P).transpose(1, 0, 2, 4, 3).reshape(SLICES, T, LANES, P)
    m = _toeplitz_expand(block_diag(kern, P).astype(bf16))

    rev_re = pw_re[:T][::-1]
    rev_im = pw_im[:T][::-1]
    e_re = rev_re[:, :, :, None] * bb_re[None] - rev_im[:, :, :, None] * bb_im[None]
    e_im = rev_re[:, :, :, None] * bb_im[None] + rev_im[:, :, :, None] * bb_re[None]

    def slice_e(e):
        e = e.reshape(T, SLICES, GPS, N, P).transpose(1, 0, 2, 4, 3).reshape(SLICES, T, LANES, N)
        return block_diag(e, P).astype(bf16).reshape(SLICES, SLICE_IN, SLICE_STATE)

    nx_re = pw_re[1:]
    nx_im = pw_im[1:]
    crt = cr.transpose(0, 2, 1)
    cit = ci.transpose(0, 2, 1)
    f_re = crt[None] * nx_re[:, :, :, None] - cit[None] * nx_im[:, :, :, None]
    f_im = -(crt[None] * nx_im[:, :, :, None] + cit[None] * nx_re[:, :, :, None])

    def slice_f(f):
        f = f.reshape(T, SLICES, SLICE_STATE, P).transpose(1, 0, 2, 3)
        f = block_diag(f, N).astype(bf16)
        return f.transpose(0, 2, 1, 3).reshape(SLICES, SLICE_STATE, SLICE_IN)

    lam_re = pw_re[T].reshape(SLICES, 1, SLICE_STATE)
    lam_im = pw_im[T].reshape(SLICES, 1, SLICE_STATE)
    return m, slice_e(e_re), slice_e(e_im), slice_f(f_re), slice_f(f_im), lam_re, lam_im


def _toeplitz_body(bd_ref, o_ref):
    s = pl.program_id(1)
    for t in range(CHUNK):
        blk = bd_ref[0, jnp.maximum(t - s, 0)]
        o_ref[0, :, t * LANES:(t + 1) * LANES] = jnp.where(t >= s, blk, jnp.zeros_like(blk))


def _toeplitz_expand(bd):
    return pl.pallas_call(
        _toeplitz_body,
        grid=(SLICES, CHUNK),
        in_specs=[pl.BlockSpec((1, CHUNK, LANES, LANES), lambda j, s: (j, 0, 0, 0))],
        out_specs=pl.BlockSpec((1, LANES, SLICE_IN), lambda j, s: (j, s, 0)),
        out_shape=jax.ShapeDtypeStruct((SLICES, SLICE_IN, SLICE_IN), bf16),
        compiler_params=_cparams(("parallel", "arbitrary")),
        name="toeplitz_expand",
    )(bd)


def _state_to_slices(h, pad_streams):
    n = h.shape[0]
    t = h.astype(f32).reshape(n, SLICES, SLICE_STATE).transpose(1, 0, 2)
    return jnp.pad(t, ((0, 0), (0, pad_streams - n), (0, 0)))[:, None]


def _state_from_slices(h):
    return h.transpose(1, 0, 2).reshape(h.shape[1], SSM_GROUPS, SSM_STATE)


def _mix0_body(u_ref, y_ref, x_ref, hist_ref, p0_ref, pw_ref, ps_ref, d_ref, wglu_ref, bglu_ref, wout_ref,
               o_ref, ext_ref, *, sb, sl):
    rows = sb * sl
    ext_ref[:, :N_META, :] = hist_ref[...]
    ext_ref[:, N_META:, :] = u_ref[:, :, :POOL_WIDTH]
    step = lax.broadcasted_iota(jnp.int32, (sb, sl, POOL_GROUP), 1).astype(f32)
    seen = p0_ref[...] + step + 1.0

    acc = x_ref[...].reshape(rows, D_MODEL)
    for gi, w in enumerate(POOL_WINDOWS):
        cols = slice(gi * POOL_GROUP, (gi + 1) * POOL_GROUP)
        tot = ext_ref[:, N_META:N_META + sl, cols]
        for k in range(1, w):
            tot = tot + ext_ref[:, N_META - k:N_META - k + sl, cols]
        mean = tot / jnp.minimum(seen, float(w))
        diff = (mean - ext_ref[:, N_META:N_META + sl, cols]).reshape(rows, POOL_GROUP)
        ya = jnp.dot(diff.astype(bf16), pw_ref[gi], preferred_element_type=f32) * ps_ref[:, cols]
        acc = acc + jnp.dot(ya.astype(bf16), wout_ref[cols, :], preferred_element_type=f32)

    us = u_ref[:, :, POOL_WIDTH:].reshape(rows, SSM_WIDTH)
    y = y_ref[...].reshape(rows, SSM_WIDTH) + d_ref[...] * us
    z = jax.nn.gelu(y)
    gate = jax.nn.sigmoid(jnp.dot(z.astype(bf16), wglu_ref[...], preferred_element_type=f32) + bglu_ref[...])
    yb = z * gate
    acc = acc + jnp.dot(yb.astype(bf16), wout_ref[POOL_WIDTH:, :], preferred_element_type=f32)
    o_ref[...] = acc.reshape(sb, sl, D_MODEL)


def _mix0(u, y, x, hist, p0, pool_w, pool_scale, ssm_d, w_glu, b_glu, w_out, sb, sl):
    nseg = u.shape[0] // sl
    u3 = u.reshape(nseg, sl, D_MODEL)
    y3 = y.reshape(nseg, sl, SSM_WIDTH)
    x3 = x.reshape(nseg, sl, D_MODEL)
    seg3 = lambda i: (i, 0, 0)
    const2 = lambda i: (0, 0)
    out = pl.pallas_call(
        functools.partial(_mix0_body, sb=sb, sl=sl),
        grid=(nseg // sb,),
        in_specs=[
            pl.BlockSpec((sb, sl, D_MODEL), seg3),
            pl.BlockSpec((sb, sl, SSM_WIDTH), seg3),
            pl.BlockSpec((sb, sl, D_MODEL), seg3),
            pl.BlockSpec((sb, N_META, POOL_WIDTH), seg3),
            pl.BlockSpec((sb, 1, POOL_GROUP), seg3),
            pl.BlockSpec((len(POOL_WINDOWS), POOL_GROUP, POOL_GROUP), lambda i: (0, 0, 0)),
            pl.BlockSpec((1, POOL_WIDTH), const2),
            pl.BlockSpec((1, SSM_WIDTH), const2),
            pl.BlockSpec((SSM_WIDTH, SSM_WIDTH), const2),
            pl.BlockSpec((1, SSM_WIDTH), const2),
            pl.BlockSpec((D_MODEL, D_MODEL), const2),
        ],
        out_specs=pl.BlockSpec((sb, sl, D_MODEL), seg3),
        out_shape=jax.ShapeDtypeStruct((nseg, sl, D_MODEL), f32),
        scratch_shapes=[pltpu.VMEM((sb, N_META + sl, POOL_WIDTH), f32)],
        compiler_params=_cparams(("parallel",)),
        name="mix0",
    )(u3, y3, x3, hist, p0, pool_w, pool_scale.reshape(1, POOL_WIDTH), ssm_d.reshape(1, SSM_WIDTH),
      w_glu, b_glu.reshape(1, SSM_WIDTH), w_out)
    return out.reshape(nseg * sl, D_MODEL)


def _sb_tiles(qs, kts, vts, tri, carries, accs, mask, group=1):
    n = len(qs)
    ss = [lax.dot_general(qs[i], kts[i], (((1,), (1,)), ((), ())), preferred_element_type=f32) for i in range(n)]
    log_betas, log_1mbs, his, los = [], [], [], []
    for s in ss:
        z = s * SB_SCALE_LOG2
        neg_abs = pltpu.bitcast(pltpu.bitcast(z, jnp.uint32) | jnp.uint32(0x80000000), f32)
        t = jnp.log(1.0 + jnp.exp2(neg_abs)) * LOG2E
        log_beta = jnp.minimum(z, 0.0) - t
        log_1mb = log_beta - z
        if mask is not None:
            log_1mb = jnp.where(mask, log_1mb, 0.0)
        hi = log_1mb.astype(bf16)
        log_betas.append(log_beta)
        log_1mbs.append(log_1mb)
        his.append(hi)
        los.append((log_1mb - hi.astype(f32)).astype(bf16))
    sums = [jnp.dot(his[i], tri, preferred_element_type=f32) + jnp.dot(los[i], tri, preferred_element_type=f32)
            for i in range(n)]
    tile_carries, out_carries = [], []
    for c in range(n // group):
        carry = carries[c]
        for i in range(c * group, (c + 1) * group):
            tile_carries.append(carry)
            carry = carry + sums[i][:, :1] + log_1mbs[i][:, :1]
        out_carries.append(carry)
    ws = []
    for i in range(n):
        w = jnp.exp2(log_betas[i] + (sums[i] + tile_carries[i]))
        if mask is not None:
            w = jnp.where(mask, w, 0.0)
        ws.append(w.astype(bf16))
    out_accs = []
    for c in range(n // group):
        acc = accs[c]
        for i in range(c * group, (c + 1) * group):
            acc = acc + jnp.dot(ws[i], vts[i], preferred_element_type=f32)
        out_accs.append(acc)
    return out_carries, out_accs


def _sb_tile(q, kt, vt, tri, carry, acc, mask):
    carries, accs = _sb_tiles([q], [kt], [vt], tri, [carry], [acc], mask)
    return carries[0], accs[0]


def _attn_prompt_body(q_ref, k_ref, v_ref, km_ref, vm_ref, tri_ref, o_ref, carry_ref, acc_ref, *, hb):
    tq = ATTN_TILE
    qi = pl.program_id(2)
    tri = tri_ref[...]
    cols = [slice(h * SB_HEAD_DIM, (h + 1) * SB_HEAD_DIM) for h in range(hb)]
    qs = [q_ref[:, c] for c in cols]

    def sweep(js, mask, first):
        g = len(js)
        r0s = [pl.multiple_of(j * tq, tq) for j in js]
        kts = [k_ref[pl.ds(r0, tq), c] for c in cols for r0 in r0s]
        vts = [v_ref[pl.ds(r0, tq), c] for c in cols for r0 in r0s]
        if first:
            carries = [jnp.zeros((tq, 1), f32)] * hb
            accs = [jnp.zeros((tq, SB_HEAD_DIM), f32)] * hb
        else:
            carries = [carry_ref[h] for h in range(hb)]
            accs = [acc_ref[h] for h in range(hb)]
        carries, accs = _sb_tiles([q for q in qs for _ in js], kts, vts, tri, carries, accs, mask, group=g)
        for h in range(hb):
            carry_ref[h] = carries[h]
            acc_ref[h] = accs[h]

    row = lax.broadcasted_iota(jnp.int32, (tq, tq), 0)
    col = lax.broadcasted_iota(jnp.int32, (tq, tq), 1)
    sweep([qi], col < row, True)

    odd = qi & 1

    @pl.when(odd == 1)
    def _():
        sweep([qi - 1], None, False)

    def body(it, c):
        top = qi - odd - 1 - 2 * it
        sweep([top, top - 1], None, False)
        return c

    lax.fori_loop(0, lax.shift_right_logical(qi, 1), body, 0)
    colm = lax.broadcasted_iota(jnp.int32, (tq, 128), 1)
    _, accs = _sb_tiles(qs, [km_ref[:, c] for c in cols], [vm_ref[:, c] for c in cols], tri_ref[:128, :128],
                        [carry_ref[h] for h in range(hb)], [acc_ref[h] for h in range(hb)], colm < N_META)
    for h in range(hb):
        o_ref[:, cols[h]] = accs[h].astype(o_ref.dtype)


def _attn_prompt(qkvb, kmeta, vmeta, tri, nb, seq):
    tq = ATTN_TILE
    nq = seq // tq
    hb = ATTN_HEADS_PER_STEP
    wd = hb * SB_HEAD_DIM
    ng = SB_HEADS // hb
    return pl.pallas_call(
        functools.partial(_attn_prompt_body, hb=hb),
        grid=(nb, ng, nq),
        in_specs=[
            pl.BlockSpec((tq, wd), lambda b, g, i: (b * nq + i, g)),
            pl.BlockSpec((seq, wd), lambda b, g, i: (b, ng + g)),
            pl.BlockSpec((seq, wd), lambda b, g, i: (b, 2 * ng + g)),
            pl.BlockSpec((128, wd), lambda b, g, i: (0, g)),
            pl.BlockSpec((128, wd), lambda b, g, i: (0, g)),
            pl.BlockSpec((tq, tq), lambda b, g, i: (0, 0)),
        ],
        out_specs=pl.BlockSpec((tq, wd), lambda b, g, i: (b * nq + i, g)),
        out_shape=jax.ShapeDtypeStruct((nb * seq, D_MODEL), bf16),
        scratch_shapes=[pltpu.VMEM((hb, tq, 1), f32), pltpu.VMEM((hb, tq, SB_HEAD_DIM), f32)],
        compiler_params=_cparams(("parallel", "parallel", "arbitrary"), ATTN_FLAGS),
        name="attn_prompt",
    )(qkvb, qkvb, qkvb, kmeta, vmeta, tri)


def _stack_heads(q):
    ln = q.shape[0]
    rows = SB_HEADS * ln
    qt = jnp.concatenate([q] * SB_HEADS, axis=0)
    rh = lax.shift_right_logical(lax.broadcasted_iota(jnp.int32, (rows, D_MODEL), 0), _log2(ln))
    ch = lax.shift_right_logical(lax.broadcasted_iota(jnp.int32, (rows, D_MODEL), 1), _log2(SB_HEAD_DIM))
    return jnp.where(rh == ch, qt, jnp.zeros_like(qt))


def _own_tile(q_ref, ko_ref, vo_ref, tri_ref, ln):
    rows = SB_HEADS * ln
    qbd = _stack_heads(q_ref[...])
    row = lax.broadcasted_iota(jnp.int32, (rows, 128), 0)
    col = lax.broadcasted_iota(jnp.int32, (rows, 128), 1)
    mask = col < (row & (ln - 1))
    carry = jnp.zeros((rows, 1), f32)
    acc = jnp.zeros((rows, D_MODEL), f32)
    carry, acc = _sb_tile(qbd, ko_ref[...], vo_ref[...], tri_ref[:128, :128], carry, acc, mask)
    return qbd, carry, acc


def _write_heads(o_ref, acc, ln):
    for h in range(SB_HEADS):
        cols = slice(h * SB_HEAD_DIM, (h + 1) * SB_HEAD_DIM)
        o_ref[:, cols] = acc[h * ln:(h + 1) * ln, cols].astype(o_ref.dtype)


def _head_major_tile(c_ref, sub):
    base = sub * ATTN_TILE * SB_HEADS
    return jnp.concatenate(
        [c_ref[pl.ds(base + h, ATTN_TILE, stride=SB_HEADS), :] for h in range(SB_HEADS)], axis=1).astype(bf16)


def _attn_cached_body(q_ref, ko_ref, vo_ref, kc_ref, vc_ref, tri_ref, o_ref, qbd_ref, carry_ref, acc_ref, *, ln, tk):
    kt = pl.program_id(1)

    @pl.when(kt == 0)
    def _():
        qbd, carry, acc = _own_tile(q_ref, ko_ref, vo_ref, tri_ref, ln)
        qbd_ref[...] = qbd
        carry_ref[...] = carry
        acc_ref[...] = acc

    qbd = qbd_ref[...]
    subs = list(reversed(range(tk // ATTN_TILE)))
    carries, accs = _sb_tiles([qbd] * len(subs), [_head_major_tile(kc_ref, s) for s in subs],
                              [_head_major_tile(vc_ref, s) for s in subs], tri_ref[...],
                              [carry_ref[...]], [acc_ref[...]], None, group=len(subs))
    carry = carries[0]
    acc = accs[0]
    carry_ref[...] = carry
    acc_ref[...] = acc

    @pl.when(kt == pl.num_programs(1) - 1)
    def _():
        _write_heads(o_ref, acc, ln)


def _attn_cached(qkvb, k_own, v_own, k_cache, v_cache, tri, nb, ln, tk):
    past = k_cache.shape[0] // (nb * SB_HEADS)
    nkt = past // tk
    rows = SB_HEADS * ln
    return pl.pallas_call(
        functools.partial(_attn_cached_body, ln=ln, tk=tk),
        grid=(nb, nkt),
        in_specs=[
            pl.BlockSpec((ln, D_MODEL), lambda b, t: (b, 0)),
            pl.BlockSpec((128, D_MODEL), lambda b, t: (b, 0)),
            pl.BlockSpec((128, D_MODEL), lambda b, t: (b, 0)),
            pl.BlockSpec((tk * SB_HEADS, SB_HEAD_DIM), lambda b, t: (b * nkt + nkt - 1 - t, 0)),
            pl.BlockSpec((tk * SB_HEADS, SB_HEAD_DIM), lambda b, t: (b * nkt + nkt - 1 - t, 0)),
            pl.BlockSpec((ATTN_TILE, ATTN_TILE), lambda b, t: (0, 0)),
        ],
        out_specs=pl.BlockSpec((ln, D_MODEL), lambda b, t: (b, 0)),
        out_shape=jax.ShapeDtypeStruct((nb * ln, D_MODEL), bf16),
        scratch_shapes=[
            pltpu.VMEM((rows, D_MODEL), bf16),
            pltpu.VMEM((rows, 1), f32),
            pltpu.VMEM((rows, D_MODEL), f32),
        ],
        compiler_params=_cparams(("parallel", "arbitrary")),
        name="attn_cached",
    )(qkvb, k_own, v_own, k_cache, v_cache, tri)


def _attn_own_body(q_ref, ko_ref, vo_ref, tri_ref, o_ref, *, ln):
    _, _, acc = _own_tile(q_ref, ko_ref, vo_ref, tri_ref, ln)
    _write_heads(o_ref, acc, ln)


def _attn_own(q, k_own, v_own, tri, ln):
    full = lambda i: (0, 0)
    return pl.pallas_call(
        functools.partial(_attn_own_body, ln=ln),
        grid=(1,),
        in_specs=[
            pl.BlockSpec((ln, D_MODEL), full),
            pl.BlockSpec((128, D_MODEL), full),
            pl.BlockSpec((128, D_MODEL), full),
            pl.BlockSpec((ATTN_TILE, ATTN_TILE), full),
        ],
        out_specs=pl.BlockSpec((ln, D_MODEL), full),
        out_shape=jax.ShapeDtypeStruct((ln, D_MODEL), bf16),
        compiler_params=_cparams(("arbitrary",)),
        name="attn_own",
    )(q, k_own, v_own, tri)


def kernel(x_prompt, x_sample, cache_pool, state_ssm_re, state_ssm_im, cache_k, cache_v, meta_tokens, ffn_norm, ffn_w_gate, ffn_w_up, ffn_w_down, mix_norm, ab_w_in, pool_w, pool_scale, ssm_a_re, ssm_a_im, ssm_log_dt, ssm_b_re, ssm_b_im, ssm_c_re, ssm_c_im, ssm_d, ssm_w_glu, ssm_b_glu, ab_w_out, sb_w_qkv, sb_w_out, final_norm):
    nb, seq, _ = x_prompt.shape
    db, dl, _ = x_sample.shape
    assert dl == N_META == CHUNK and seq % ATTN_TILE == 0
    n_small = db + 1
    rows_s = n_small * dl
    tm_p = TM_PROMPT

    def ffn(xp, xs, which, final_gamma=None):
        xs, wg, wu, wd = _ffn(xs, ffn_norm[which], ffn_w_gate, ffn_w_up, ffn_w_down, rows_s, FFN_TILE_F_SMALL,
                              final_gamma, which)
        xp = _ffn(xp, ffn_norm[which], wg, wu, wd, TM_FFN, FFN_TILE_F, final_gamma)
        return xp, xs

    w_in = ab_w_in[0].astype(bf16)
    w_out0 = ab_w_out[0].astype(bf16)
    w_pool = pool_w[0].astype(bf16)
    w_glu = ssm_w_glu[0].astype(bf16)
    w_qkv = sb_w_qkv[0].astype(bf16)
    w_out1 = sb_w_out[0].astype(bf16)
    mats = _ssm_matrices(ssm_a_re[0], ssm_a_im[0], ssm_log_dt[0], ssm_b_re[0], ssm_b_im[0],
                         ssm_c_re[0], ssm_c_im[0])
    ti = jnp.arange(ATTN_TILE)
    tri = (ti[:, None] > ti[None, :]).astype(bf16)

    xp = x_prompt.reshape(nb * seq, D_MODEL)
    xs = jnp.concatenate([x_sample.reshape(db * dl, D_MODEL), meta_tokens.astype(f32)], axis=0)

    xp, xs = ffn(xp, xs, (0, 0))
    up = _norm_proj(xp, mix_norm[0], w_in, tm_p, D_MODEL, False)
    us = _norm_proj(xs, mix_norm[0], w_in, rows_s, D_MODEL, False)

    pad_s = -(-n_small // 8) * 8
    zero_state = jnp.zeros((1, SSM_GROUPS, SSM_STATE), f32)
    h0r = _state_to_slices(jnp.concatenate([state_ssm_re[0], zero_state], axis=0), pad_s)
    h0i = _state_to_slices(jnp.concatenate([state_ssm_im[0], zero_state], axis=0), pad_s)
    us_pad = jnp.pad(us, ((0, (pad_s - n_small) * dl), (0, 0)))
    ys, hsr, hsi = _ssm_scan(us_pad, h0r, h0i, mats, 1, pad_s, 1)
    ys = ys[:rows_s]
    hsr = hsr[:, 0]
    hsi = hsi[:, 0]
    us3 = us.reshape(n_small, dl, D_MODEL)
    hist_s = jnp.concatenate([
        jnp.concatenate([jnp.zeros((db, 1, POOL_WIDTH), f32), cache_pool[0].astype(f32)], axis=1),
        jnp.zeros((1, N_META, POOL_WIDTH), f32)], axis=0)
    p0_s = jnp.concatenate([jnp.full((db, 1, POOL_GROUP), float(N_META), f32),
                            jnp.zeros((1, 1, POOL_GROUP), f32)], axis=0)
    xs = _mix0(us, ys, xs, hist_s, p0_s, w_pool, pool_scale[0], ssm_d[0], w_glu, ssm_b_glu[0], w_out0,
               n_small, dl)

    meta_r = hsr[:, db].reshape(SLICES, 1, 1, SLICE_STATE)
    meta_i = hsi[:, db].reshape(SLICES, 1, 1, SLICE_STATE)
    yp, hpr, hpi = _ssm_scan(up, meta_r, meta_i, mats, nb, 1, seq // CHUNK)
    sl_p = MIX0_SEG
    up4 = up.reshape(nb, seq // sl_p, sl_p, D_MODEL)
    meta_tail = jnp.broadcast_to(us3[db:, :, :POOL_WIDTH], (nb, N_META, POOL_WIDTH))
    hist_p = jnp.concatenate([meta_tail[:, None], up4[:, :-1, sl_p - N_META:, :POOL_WIDTH]], axis=1)
    hist_p = hist_p.reshape(nb * (seq // sl_p), N_META, POOL_WIDTH)
    p0_p = jnp.full((nb * (seq // sl_p), 1, POOL_GROUP), float(N_META), f32)
    xp = _mix0(up, yp, xp, hist_p, p0_p, w_pool, pool_scale[0], ssm_d[0], w_glu, ssm_b_glu[0], w_out0, 1, sl_p)

    xp, xs = ffn(xp, xs, (0, 1))

    xp, xs = ffn(xp, xs, (1, 0))
    qkv_s, qkvb_s = _norm_proj(xs, mix_norm[1], w_qkv, rows_s, 1024, True)
    qkvb_p, kp, vp = _qkv_prompt(xp, mix_norm[1], w_qkv, qkv_s[db * dl:], nb, seq)

    kcol = slice(D_MODEL, 2 * D_MODEL)
    vcol = slice(2 * D_MODEL, 3 * D_MODEL)
    pad_keys = lambda a: jnp.pad(a, ((0, 0), (0, 128 - dl), (0, 0))).reshape(-1, D_MODEL)
    kb3 = qkvb_s[:, kcol].reshape(n_small, dl, D_MODEL)
    vb3 = qkvb_s[:, vcol].reshape(n_small, dl, D_MODEL)
    k_own = pad_keys(kb3[:db])
    v_own = pad_keys(vb3[:db])
    k_meta = pad_keys(kb3[db:])
    v_meta = pad_keys(vb3[db:])

    past = cache_k.shape[2]
    o_sample = _attn_cached(qkvb_s, k_own, v_own, cache_k.reshape(db * past * SB_HEADS, SB_HEAD_DIM),
                            cache_v.reshape(db * past * SB_HEADS, SB_HEAD_DIM), tri, db, dl, CACHE_TILE)
    o_meta = _attn_own(qkvb_s[db * dl:, :D_MODEL], k_meta, v_meta, tri, dl)
    o_prompt = _attn_prompt(qkvb_p, k_meta, v_meta, tri, nb, seq)
    xp = _proj_res(o_prompt, w_out1, xp, tm_p)
    xs = _proj_res(jnp.concatenate([o_sample, o_meta], axis=0), w_out1, xs, rows_s)

    yp_out, ys_out = ffn(xp, xs, (1, 1), final_norm)

    y_prompt = yp_out.reshape(nb, seq, D_MODEL)
    y_sample = ys_out[:db * dl].reshape(db, dl, D_MODEL)
    up3 = up.reshape(nb, seq, D_MODEL)
    pool_p = up3[:, seq - POOL_HIST:, :POOL_WIDTH][None]
    pool_s = us3[:db, dl - POOL_HIST:, :POOL_WIDTH][None]
    re_p = _state_from_slices(hpr[:, :, 0])[None]
    im_p = _state_from_slices(hpi[:, :, 0])[None]
    re_s = _state_from_slices(hsr[:, :db])[None]
    im_s = _state_from_slices(hsi[:, :db])[None]
    heads = lambda a, n: a.reshape(n, -1, SB_HEADS, SB_HEAD_DIM)
    k3 = qkv_s[:, kcol].reshape(n_small, dl, D_MODEL)
    v3 = qkv_s[:, vcol].reshape(n_small, dl, D_MODEL)
    k_p = kp[None]
    v_p = vp[None]
    k_s = heads(k3[:db], db)[None]
    v_s = heads(v3[:db], db)[None]
    return (y_prompt, y_sample, pool_p, pool_s, re_p, im_p, re_s, im_s, k_p, v_p, k_s, v_s)
```

```python
import functools
import math

import jax
import jax.numpy as jnp
from jax import lax
from jax.experimental import pallas as pl
from jax.experimental.pallas import tpu as pltpu

f32 = jnp.float32
bf16 = jnp.bfloat16

D_MODEL = 2048
N_META = 16
RMS_EPS = 1e-6
FFN_RESIDUAL = 0.5
FFN_DIM = 5632
POOL_WIDTH = 1024
POOL_WINDOWS = (2, 4, 8, 16)
POOL_GROUP = 256
POOL_HIST = 15
SSM_WIDTH = 1024
SSM_GROUP = 16
SSM_GROUPS = 64
SSM_STATE = 64
SB_HEADS = 16
SB_HEAD_DIM = 128
LOG2E = 1.4426950408889634
SB_SCALE_LOG2 = LOG2E / math.sqrt(SB_HEAD_DIM)

LANES = 128
CHUNK = 16
SLICES = SSM_WIDTH // LANES
GPS = LANES // SSM_GROUP
SLICE_IN = CHUNK * LANES
SLICE_STATE = GPS * SSM_STATE

VMEM_LIMIT = 56 * 1024 * 1024
ATTN_TILE = 256
ATTN_HEADS_PER_STEP = 8
ATTN_FLAGS = None
FFN_TILE_F = 512
FFN_TILE_F_SMALL = 256
TM_PROMPT = 512
TM_FFN = 1024
MIX0_SEG = 256
CACHE_TILE = 1024


def _cparams(sem, flags=None):
    return pltpu.CompilerParams(dimension_semantics=sem, vmem_limit_bytes=VMEM_LIMIT, flags=flags)


def _log2(n):
    assert n & (n - 1) == 0
    return n.bit_length() - 1


def _rms(xf, g):
    ms = jnp.mean(xf * xf, axis=-1, keepdims=True)
    return xf * lax.rsqrt(ms + RMS_EPS) * g


def _ffn_body(*refs, final, cast):
    refs = list(refs)
    x_ref, g_ref, wg_ref, wu_ref, wd_ref = refs[:5]
    gf_ref = refs[5] if final else None
    o_ref = refs[5 + final]
    h_ref = refs[-1]
    j = pl.program_id(1)

    @pl.when(j == 0)
    def _():
        xf = x_ref[...]
        h_ref[...] = _rms(xf, g_ref[...]).astype(bf16)
        o_ref[...] = xf

    wg, wu, wd = wg_ref[...], wu_ref[...], wd_ref[...]
    if cast:
        wgb_ref, wub_ref, wdb_ref = refs[6 + final:9 + final]
        wg, wu, wd = wg.astype(bf16), wu.astype(bf16), wd.astype(bf16)
        wgb_ref[...] = wg
        wub_ref[...] = wu
        wdb_ref[...] = wd

    h = h_ref[...]
    g = jnp.dot(h, wg, preferred_element_type=f32)
    u = jnp.dot(h, wu, preferred_element_type=f32)
    a = (jax.nn.silu(g) * u * FFN_RESIDUAL).astype(bf16)
    o_ref[...] += jnp.dot(a, wd, preferred_element_type=f32)

    if final:
        @pl.when(j == pl.num_programs(1) - 1)
        def _():
            o_ref[...] = _rms(o_ref[...], gf_ref[...])


def _ffn(x, gamma, wg, wu, wd, tm, tf, final_gamma=None, which=None):
    rows = x.shape[0]
    final = final_gamma is not None
    cast = which is not None
    if cast:
        assert rows == tm
        la, po = which
        w_specs = [
            pl.BlockSpec((None, None, D_MODEL, tf), lambda i, j: (la, po, 0, j)),
            pl.BlockSpec((None, None, D_MODEL, tf), lambda i, j: (la, po, 0, j)),
            pl.BlockSpec((None, None, tf, D_MODEL), lambda i, j: (la, po, j, 0)),
        ]
    else:
        w_specs = [
            pl.BlockSpec((D_MODEL, tf), lambda i, j: (0, j)),
            pl.BlockSpec((D_MODEL, tf), lambda i, j: (0, j)),
            pl.BlockSpec((tf, D_MODEL), lambda i, j: (j, 0)),
        ]
    in_specs = [pl.BlockSpec((tm, D_MODEL), lambda i, j: (i, 0)), pl.BlockSpec((1, D_MODEL), lambda i, j: (0, 0))]
    in_specs += w_specs
    args = [x, gamma.reshape(1, D_MODEL), wg, wu, wd]
    if final:
        in_specs.append(pl.BlockSpec((1, D_MODEL), lambda i, j: (0, 0)))
        args.append(final_gamma.reshape(1, D_MODEL))
    out_specs = [pl.BlockSpec((tm, D_MODEL), lambda i, j: (i, 0))]
    out_shape = [jax.ShapeDtypeStruct((rows, D_MODEL), f32)]
    if cast:
        out_specs += [
            pl.BlockSpec((D_MODEL, tf), lambda i, j: (0, j)),
            pl.BlockSpec((D_MODEL, tf), lambda i, j: (0, j)),
            pl.BlockSpec((tf, D_MODEL), lambda i, j: (j, 0)),
        ]
        out_shape += [
            jax.ShapeDtypeStruct((D_MODEL, FFN_DIM), bf16),
            jax.ShapeDtypeStruct((D_MODEL, FFN_DIM), bf16),
            jax.ShapeDtypeStruct((FFN_DIM, D_MODEL), bf16),
        ]
    res = pl.pallas_call(
        functools.partial(_ffn_body, final=final, cast=cast),
        grid=(rows // tm, FFN_DIM // tf),
        in_specs=in_specs,
        out_specs=out_specs,
        out_shape=out_shape,
        scratch_shapes=[pltpu.VMEM((tm, D_MODEL), bf16)],
        compiler_params=_cparams(("arbitrary", "arbitrary")),
        name=("ffn_final" if final else "ffn") + ("_cast" if cast else ""),
    )(*args)
    return res if cast else res[0]


def _norm_proj_body(x_ref, g_ref, w_ref, *refs, with_bf16):
    if with_bf16:
        o_ref, ob_ref, h_ref = refs
    else:
        o_ref, h_ref = refs

    @pl.when(pl.program_id(1) == 0)
    def _():
        h_ref[...] = _rms(x_ref[...], g_ref[...]).astype(bf16)

    r = jnp.dot(h_ref[...], w_ref[...], preferred_element_type=f32)
    o_ref[...] = r
    if with_bf16:
        ob_ref[...] = r.astype(bf16)


def _norm_proj(x, gamma, w, tm, tn, with_bf16):
    rows = x.shape[0]
    n = w.shape[1]
    out_shape = [jax.ShapeDtypeStruct((rows, n), f32)]
    out_specs = [pl.BlockSpec((tm, tn), lambda i, j: (i, j))]
    if with_bf16:
        out_shape.append(jax.ShapeDtypeStruct((rows, n), bf16))
        out_specs.append(pl.BlockSpec((tm, tn), lambda i, j: (i, j)))
    res = pl.pallas_call(
        functools.partial(_norm_proj_body, with_bf16=with_bf16),
        grid=(rows // tm, n // tn),
        in_specs=[
            pl.BlockSpec((tm, D_MODEL), lambda i, j: (i, 0)),
            pl.BlockSpec((1, D_MODEL), lambda i, j: (0, 0)),
            pl.BlockSpec((D_MODEL, tn), lambda i, j: (0, j)),
        ],
        out_specs=out_specs,
        out_shape=out_shape,
        scratch_shapes=[pltpu.VMEM((tm, D_MODEL), bf16)],
        compiler_params=_cparams(("parallel", "arbitrary")),
        name="norm_proj_qkv" if with_bf16 else "norm_proj",
    )(x, gamma.reshape(1, D_MODEL), w)
    return res if with_bf16 else res[0]


def _qkv_prompt_body(x_ref, g_ref, w_ref, mkv_ref, qkvb_ref, kp_ref, vp_ref, h_ref, stage_ref, sem_ref, msem_ref,
                     *, tm, tn, tpb):
    i = pl.program_id(0)
    j = pl.program_id(1)
    last_i = pl.num_programs(0) - 1
    nj = 3 * D_MODEL // tn
    njq = D_MODEL // tn
    hpt = tn // SB_HEAD_DIM

    @pl.when(j == 0)
    def _():
        h_ref[...] = _rms(x_ref[...], g_ref[...]).astype(bf16)

    r = jnp.dot(h_ref[...], w_ref[:, pl.ds(pl.multiple_of(j * tn, tn), tn)], preferred_element_type=f32)
    qkvb_ref[...] = r.astype(bf16)

    b = lax.shift_right_logical(i, _log2(tpb))
    il = i & (tpb - 1)
    t0 = N_META + il * tm
    n = i * (nj - njq) + (j - njq)
    slot = n & 1

    def copies(dst_ref, sl, head0):
        return [pltpu.make_async_copy(stage_ref.at[sl, :, pl.ds(hh * SB_HEAD_DIM, SB_HEAD_DIM)],
                                      dst_ref.at[b, pl.ds(t0, tm), head0 + hh, :], sem_ref.at[sl])
                for hh in range(hpt)]

    def meta_copies(dst_ref, head0):
        return [pltpu.make_async_copy(mkv_ref.at[:, pl.ds(hh * SB_HEAD_DIM, SB_HEAD_DIM)],
                                      dst_ref.at[b, pl.ds(0, N_META), head0 + hh, :], msem_ref)
                for hh in range(hpt)]

    def send(dst_ref, head0):
        for c in copies(dst_ref, slot, head0):
            c.start()

        @pl.when(il == 0)
        def _():
            cs = meta_copies(dst_ref, head0)
            for c in cs:
                c.start()
            for c in cs:
                c.wait()

    @pl.when(j >= njq)
    def _():
        @pl.when(n >= 2)
        def _():
            for c in copies(kp_ref, slot, 0):
                c.wait()

        stage_ref[slot] = r

        @pl.when(j < 2 * njq)
        def _():
            send(kp_ref, (j - njq) * hpt)

        @pl.when(j >= 2 * njq)
        def _():
            send(vp_ref, (j - 2 * njq) * hpt)

        @pl.when(jnp.logical_and(i == last_i, j == nj - 1))
        def _():
            for c in copies(kp_ref, slot, 0) + copies(kp_ref, 1 - slot, 0):
                c.wait()


def _qkv_prompt(x, gamma, w, meta_kv, nb, seq):
    tm, tn = TM_PROMPT, 1024
    tpb = seq // tm
    rows = nb * seq
    kv_shape = jax.ShapeDtypeStruct((nb, N_META + seq, SB_HEADS, SB_HEAD_DIM), f32)
    return pl.pallas_call(
        functools.partial(_qkv_prompt_body, tm=tm, tn=tn, tpb=tpb),
        grid=(rows // tm, 3 * D_MODEL // tn),
        in_specs=[
            pl.BlockSpec((tm, D_MODEL), lambda i, j: (i, 0)),
            pl.BlockSpec((1, D_MODEL), lambda i, j: (0, 0)),
            pl.BlockSpec((D_MODEL, 3 * D_MODEL), lambda i, j: (0, 0), pipeline_mode=pl.Buffered(1)),
            pl.BlockSpec((N_META, tn), lambda i, j: (0, j)),
        ],
        out_specs=[
            pl.BlockSpec((tm, tn), lambda i, j: (i, j)),
            pl.BlockSpec(memory_space=pl.ANY),
            pl.BlockSpec(memory_space=pl.ANY),
        ],
        out_shape=[jax.ShapeDtypeStruct((rows, 3 * D_MODEL), bf16), kv_shape, kv_shape],
        scratch_shapes=[
            pltpu.VMEM((tm, D_MODEL), bf16),
            pltpu.VMEM((2, tm, tn), f32),
            pltpu.SemaphoreType.DMA((2,)),
            pltpu.SemaphoreType.DMA(()),
        ],
        compiler_params=_cparams(("arbitrary", "arbitrary")),
        name="qkv_prompt",
    )(x, gamma.reshape(1, D_MODEL), w, meta_kv)


def _proj_res_body(a_ref, w_ref, x_ref, o_ref):
    o_ref[...] = x_ref[...] + jnp.dot(a_ref[...], w_ref[...], preferred_element_type=f32)


def _proj_res(a, w, x, tm):
    rows = x.shape[0]
    return pl.pallas_call(
        _proj_res_body,
        grid=(rows // tm,),
        in_specs=[
            pl.BlockSpec((tm, D_MODEL), lambda i: (i, 0)),
            pl.BlockSpec((D_MODEL, D_MODEL), lambda i: (0, 0)),
            pl.BlockSpec((tm, D_MODEL), lambda i: (i, 0)),
        ],
        out_specs=pl.BlockSpec((tm, D_MODEL), lambda i: (i, 0)),
        out_shape=jax.ShapeDtypeStruct((rows, D_MODEL), f32),
        compiler_params=_cparams(("parallel",)),
        name="proj_res",
    )(a, w, x)


def _ssm_body(u_ref, h0r_ref, h0i_ref, m_ref, er_ref, ei_ref, fr_ref, fi_ref, lr_ref, li_ref,
              y_ref, hTr_ref, hTi_ref, dr_ref, di_ref, sr_ref, si_ref, *, ns, nch):
    nc = ns * nch
    lhs = jnp.concatenate([u_ref[pl.ds(s, nc, stride=CHUNK), :].astype(bf16) for s in range(CHUNK)], axis=1)
    dr_ref[...] = jnp.dot(lhs, er_ref[0], preferred_element_type=f32)
    di_ref[...] = jnp.dot(lhs, ei_ref[0], preferred_element_type=f32)
    lr = lr_ref[0]
    li = li_ref[0]
    h0r = h0r_ref[0, 0]
    h0i = h0i_ref[0, 0]
    if nch == 1:
        sr_ref[...] = h0r
        si_ref[...] = h0i
        hr = lr * h0r - li * h0i + dr_ref[...]
        hi = lr * h0i + li * h0r + di_ref[...]
    else:
        assert ns == 1

        def step(c, carry):
            hr, hi = carry
            sr_ref[pl.ds(c, 1), :] = hr
            si_ref[pl.ds(c, 1), :] = hi
            return (lr * hr - li * hi + dr_ref[pl.ds(c, 1), :], lr * hi + li * hr + di_ref[pl.ds(c, 1), :])

        hr, hi = lax.fori_loop(0, nch, step, (h0r, h0i), unroll=8)
    hTr_ref[0, 0] = hr
    hTi_ref[0, 0] = hi

    y = (jnp.dot(lhs, m_ref[0], preferred_element_type=f32)
         + jnp.dot(sr_ref[...].astype(bf16), fr_ref[0], preferred_element_type=f32)
         + jnp.dot(si_ref[...].astype(bf16), fi_ref[0], preferred_element_type=f32))
    for t in range(CHUNK):
        y_ref[pl.ds(t, nc, stride=CHUNK), :] = y[:, t * LANES:(t + 1) * LANES]


def _ssm_scan(u, h0r, h0i, mats, nb, ns, nch):
    m, er, ei, fr, fi, lr, li = mats
    rows_b = ns * nch * CHUNK
    nc = ns * nch
    w3 = lambda j, b: (j, 0, 0)
    h4 = lambda j, b: (j, 0, 0, 0)
    return pl.pallas_call(
        functools.partial(_ssm_body, ns=ns, nch=nch),
        grid=(SLICES, nb),
        in_specs=[
            pl.BlockSpec((rows_b, LANES), lambda j, b: (b, SLICES + j)),
            pl.BlockSpec((1, 1, ns, SLICE_STATE), h4),
            pl.BlockSpec((1, 1, ns, SLICE_STATE), h4),
            pl.BlockSpec((1, SLICE_IN, SLICE_IN), w3),
            pl.BlockSpec((1, SLICE_IN, SLICE_STATE), w3),
            pl.BlockSpec((1, SLICE_IN, SLICE_STATE), w3),
            pl.BlockSpec((1, SLICE_STATE, SLICE_IN), w3),
            pl.BlockSpec((1, SLICE_STATE, SLICE_IN), w3),
            pl.BlockSpec((1, 1, SLICE_STATE), w3),
            pl.BlockSpec((1, 1, SLICE_STATE), w3),
        ],
        out_specs=[
            pl.BlockSpec((rows_b, LANES), lambda j, b: (b, j)),
            pl.BlockSpec((1, 1, ns, SLICE_STATE), lambda j, b: (j, b, 0, 0)),
            pl.BlockSpec((1, 1, ns, SLICE_STATE), lambda j, b: (j, b, 0, 0)),
        ],
        out_shape=[
            jax.ShapeDtypeStruct((nb * rows_b, SSM_WIDTH), f32),
            jax.ShapeDtypeStruct((SLICES, nb, ns, SLICE_STATE), f32),
            jax.ShapeDtypeStruct((SLICES, nb, ns, SLICE_STATE), f32),
        ],
        scratch_shapes=[pltpu.VMEM((nc, SLICE_STATE), f32)] * 4,
        compiler_params=_cparams(("arbitrary", "arbitrary")),
        name="ssm_scan",
    )(u, h0r, h0i, m, er, ei, fr, fi, lr, li)


def _ssm_matrices(a_re, a_im, log_dt, b_re, b_im, c_re, c_im):
    hp = lax.Precision.HIGHEST
    N, P, T = SSM_STATE, SSM_GROUP, CHUNK
    a_re = a_re.astype(f32)
    a_im = a_im.astype(f32)
    dt = jnp.exp(log_dt.astype(f32))[:, None]
    mag = jnp.exp(a_re * dt)
    lb_re = mag * jnp.cos(a_im * dt)
    lb_im = mag * jnp.sin(a_im * dt)
    den = a_re * a_re + a_im * a_im
    k_re = ((lb_re - 1.0) * a_re + lb_im * a_im) / den
    k_im = (lb_im * a_re - (lb_re - 1.0) * a_im) / den
    b_re = b_re.astype(f32)
    b_im = b_im.astype(f32)
    bb_re = k_re[..., None] * b_re - k_im[..., None] * b_im
    bb_im = k_re[..., None] * b_im + k_im[..., None] * b_re
    cr = c_re.astype(f32)
    ci = c_im.astype(f32)

    def pw_step(c, _):
        re, im = c
        return (re * lb_re - im * lb_im, re * lb_im + im * lb_re), (re, im)

    _, (pw_re, pw_im) = lax.scan(pw_step, (jnp.ones_like(lb_re), jnp.zeros_like(lb_re)), None, length=T + 1)

    def block_diag(x, rg):
        rows, w = x.shape[-2:]
        rep = jnp.tile(jnp.eye(w, dtype=f32), (1, GPS))
        same = (jnp.arange(rows)[:, None] // rg) == (jnp.arange(GPS * w)[None, :] // w)
        return jnp.where(same, jnp.matmul(x, rep, precision=hp), 0.0)

    x_re = cr[:, :, :, None] * bb_re[:, None] - ci[:, :, :, None] * bb_im[:, None]
    x_im = cr[:, :, :, None] * bb_im[:, None] + ci[:, :, :, None] * bb_re[:, None]
    kern = (jnp.einsum('tgn,gqnp->tgqp', pw_re[:T], x_re, precision=hp)
            - jnp.einsum('tgn,gqnp->tgqp', pw_im[:T], x_im, precision=hp))
    kern = kern.reshape(T, SLICES, GPS, P, P).transpose(1, 0, 2, 4, 3).reshape(SLICES, T, LANES, P)
    m = _toeplitz_expand(block_diag(kern, P).astype(bf16))

    rev_re = pw_re[:T][::-1]
    rev_im = pw_im[:T][::-1]
    e_re = rev_re[:, :, :, None] * bb_re[None] - rev_im[:, :, :, None] * bb_im[None]
    e_im = rev_re[:, :, :, None] * bb_im[None] + rev_im[:, :, :, None] * bb_re[None]

    def slice_e(e):
        e = e.reshape(T, SLICES, GPS, N, P).transpose(1, 0, 2, 4, 3).reshape(SLICES, T, LANES, N)
        return block_diag(e, P).astype(bf16).reshape(SLICES, SLICE_IN, SLICE_STATE)

    nx_re = pw_re[1:]
    nx_im = pw_im[1:]
    crt = cr.transpose(0, 2, 1)
    cit = ci.transpose(0, 2, 1)
    f_re = crt[None] * nx_re[:, :, :, None] - cit[None] * nx_im[:, :, :, None]
    f_im = -(crt[None] * nx_im[:, :, :, None] + cit[None] * nx_re[:, :, :, None])

    def slice_f(f):
        f = f.reshape(T, SLICES, SLICE_STATE, P).transpose(1, 0, 2, 3)
        f = block_diag(f, N).astype(bf16)
        return f.transpose(0, 2, 1, 3).reshape(SLICES, SLICE_STATE, SLICE_IN)

    lam_re = pw_re[T].reshape(SLICES, 1, SLICE_STATE)
    lam_im = pw_im[T].reshape(SLICES, 1, SLICE_STATE)
    return m, slice_e(e_re), slice_e(e_im), slice_f(f_re), slice_f(f_im), lam_re, lam_im


def _toeplitz_body(bd_ref, o_ref):
    s = pl.program_id(1)
    for t in range(CHUNK):
        blk = bd_ref[0, jnp.maximum(t - s, 0)]
        o_ref[0, :, t * LANES:(t + 1) * LANES] = jnp.where(t >= s, blk, jnp.zeros_like(blk))


def _toeplitz_expand(bd):
    return pl.pallas_call(
        _toeplitz_body,
        grid=(SLICES, CHUNK),
        in_specs=[pl.BlockSpec((1, CHUNK, LANES, LANES), lambda j, s: (j, 0, 0, 0))],
        out_specs=pl.BlockSpec((1, LANES, SLICE_IN), lambda j, s: (j, s, 0)),
        out_shape=jax.ShapeDtypeStruct((SLICES, SLICE_IN, SLICE_IN), bf16),
        compiler_params=_cparams(("parallel", "arbitrary")),
        name="toeplitz_expand",
    )(bd)


def _state_to_slices(h, pad_streams):
    n = h.shape[0]
    t = h.astype(f32).reshape(n, SLICES, SLICE_STATE).transpose(1, 0, 2)
    return jnp.pad(t, ((0, 0), (0, pad_streams - n), (0, 0)))[:, None]


def _state_from_slices(h):
    return h.transpose(1, 0, 2).reshape(h.shape[1], SSM_GROUPS, SSM_STATE)


def _mix0_body(u_ref, y_ref, x_ref, hist_ref, p0_ref, pw_ref, ps_ref, d_ref, wglu_ref, bglu_ref, wout_ref,
               o_ref, ext_ref, *, sb, sl):
    rows = sb * sl
    ext_ref[:, :N_META, :] = hist_ref[...]
    ext_ref[:, N_META:, :] = u_ref[:, :, :POOL_WIDTH]
    step = lax.broadcasted_iota(jnp.int32, (sb, sl, POOL_GROUP), 1).astype(f32)
    seen = p0_ref[...] + step + 1.0

    acc = x_ref[...].reshape(rows, D_MODEL)
    for gi, w in enumerate(POOL_WINDOWS):
        cols = slice(gi * POOL_GROUP, (gi + 1) * POOL_GROUP)
        tot = ext_ref[:, N_META:N_META + sl, cols]
        for k in range(1, w):
            tot = tot + ext_ref[:, N_META - k:N_META - k + sl, cols]
        mean = tot / jnp.minimum(seen, float(w))
        diff = (mean - ext_ref[:, N_META:N_META + sl, cols]).reshape(rows, POOL_GROUP)
        ya = jnp.dot(diff.astype(bf16), pw_ref[gi], preferred_element_type=f32) * ps_ref[:, cols]
        acc = acc + jnp.dot(ya.astype(bf16), wout_ref[cols, :], preferred_element_type=f32)

    us = u_ref[:, :, POOL_WIDTH:].reshape(rows, SSM_WIDTH)
    y = y_ref[...].reshape(rows, SSM_WIDTH) + d_ref[...] * us
    z = jax.nn.gelu(y)
    gate = jax.nn.sigmoid(jnp.dot(z.astype(bf16), wglu_ref[...], preferred_element_type=f32) + bglu_ref[...])
    yb = z * gate
    acc = acc + jnp.dot(yb.astype(bf16), wout_ref[POOL_WIDTH:, :], preferred_element_type=f32)
    o_ref[...] = acc.reshape(sb, sl, D_MODEL)


def _mix0(u, y, x, hist, p0, pool_w, pool_scale, ssm_d, w_glu, b_glu, w_out, sb, sl):
    nseg = u.shape[0] // sl
    u3 = u.reshape(nseg, sl, D_MODEL)
    y3 = y.reshape(nseg, sl, SSM_WIDTH)
    x3 = x.reshape(nseg, sl, D_MODEL)
    seg3 = lambda i: (i, 0, 0)
    const2 = lambda i: (0, 0)
    out = pl.pallas_call(
        functools.partial(_mix0_body, sb=sb, sl=sl),
        grid=(nseg // sb,),
        in_specs=[
            pl.BlockSpec((sb, sl, D_MODEL), seg3),
            pl.BlockSpec((sb, sl, SSM_WIDTH), seg3),
            pl.BlockSpec((sb, sl, D_MODEL), seg3),
            pl.BlockSpec((sb, N_META, POOL_WIDTH), seg3),
            pl.BlockSpec((sb, 1, POOL_GROUP), seg3),
            pl.BlockSpec((len(POOL_WINDOWS), POOL_GROUP, POOL_GROUP), lambda i: (0, 0, 0)),
            pl.BlockSpec((1, POOL_WIDTH), const2),
            pl.BlockSpec((1, SSM_WIDTH), const2),
            pl.BlockSpec((SSM_WIDTH, SSM_WIDTH), const2),
            pl.BlockSpec((1, SSM_WIDTH), const2),
            pl.BlockSpec((D_MODEL, D_MODEL), const2),
        ],
        out_specs=pl.BlockSpec((sb, sl, D_MODEL), seg3),
        out_shape=jax.ShapeDtypeStruct((nseg, sl, D_MODEL), f32),
        scratch_shapes=[pltpu.VMEM((sb, N_META + sl, POOL_WIDTH), f32)],
        compiler_params=_cparams(("parallel",)),
        name="mix0",
    )(u3, y3, x3, hist, p0, pool_w, pool_scale.reshape(1, POOL_WIDTH), ssm_d.reshape(1, SSM_WIDTH),
      w_glu, b_glu.reshape(1, SSM_WIDTH), w_out)
    return out.reshape(nseg * sl, D_MODEL)


def _sb_tiles(qs, kts, vts, tri, carries, accs, mask, group=1):
    n = len(qs)
    ss = [lax.dot_general(qs[i], kts[i], (((1,), (1,)), ((), ())), preferred_element_type=f32) for i in range(n)]
    log_betas, log_1mbs, his = [], [], []
    for s in ss:
        z = s * SB_SCALE_LOG2
        neg_abs = pltpu.bitcast(pltpu.bitcast(z, jnp.uint32) | jnp.uint32(0x80000000), f32)
        t = jnp.log(1.0 + jnp.exp2(neg_abs)) * LOG2E
        log_beta = jnp.minimum(z, 0.0) - t
        log_1mb = log_beta - z
        if mask is not None:
            log_1mb = jnp.where(mask, log_1mb, 0.0)
        log_betas.append(log_beta)
        log_1mbs.append(log_1mb)
        his.append(log_1mb.astype(bf16))
    sums = [jnp.dot(his[i], tri, preferred_element_type=f32) for i in range(n)]
    tile_carries, out_carries = [], []
    for c in range(n // group):
        carry = carries[c]
        for i in range(c * group, (c + 1) * group):
            tile_carries.append(carry)
            carry = carry + sums[i][:, :1] + log_1mbs[i][:, :1]
        out_carries.append(carry)
    ws = []
    for i in range(n):
        w = jnp.exp2(log_betas[i] + (sums[i] + tile_carries[i]))
        if mask is not None:
            w = jnp.where(mask, w, 0.0)
        ws.append(w.astype(bf16))
    out_accs = []
    for c in range(n // group):
        acc = accs[c]
        for i in range(c * group, (c + 1) * group):
            acc = acc + jnp.dot(ws[i], vts[i], preferred_element_type=f32)
        out_accs.append(acc)
    return out_carries, out_accs


def _sb_tile(q, kt, vt, tri, carry, acc, mask):
    carries, accs = _sb_tiles([q], [kt], [vt], tri, [carry], [acc], mask)
    return carries[0], accs[0]


def _attn_prompt_body(q_ref, k_ref, v_ref, km_ref, vm_ref, tri_ref, o_ref, carry_ref, acc_ref, *, hb):
    tq = ATTN_TILE
    qi = pl.program_id(2)
    tri = tri_ref[...]
    cols = [slice(h * SB_HEAD_DIM, (h + 1) * SB_HEAD_DIM) for h in range(hb)]
    qs = [q_ref[:, c] for c in cols]

    def sweep(js, mask, first):
        g = len(js)
        r0s = [pl.multiple_of(j * tq, tq) for j in js]
        kts = [k_ref[pl.ds(r0, tq), c] for c in cols for r0 in r0s]
        vts = [v_ref[pl.ds(r0, tq), c] for c in cols for r0 in r0s]
        if first:
            carries = [jnp.zeros((tq, 1), f32)] * hb
            accs = [jnp.zeros((tq, SB_HEAD_DIM), f32)] * hb
        else:
            carries = [carry_ref[h] for h in range(hb)]
            accs = [acc_ref[h] for h in range(hb)]
        carries, accs = _sb_tiles([q for q in qs for _ in js], kts, vts, tri, carries, accs, mask, group=g)
        for h in range(hb):
            carry_ref[h] = carries[h]
            acc_ref[h] = accs[h]

    row = lax.broadcasted_iota(jnp.int32, (tq, tq), 0)
    col = lax.broadcasted_iota(jnp.int32, (tq, tq), 1)
    sweep([qi], col < row, True)

    odd = qi & 1

    @pl.when(odd == 1)
    def _():
        sweep([qi - 1], None, False)

    def body(it, c):
        top = qi - odd - 1 - 2 * it
        sweep([top, top - 1], None, False)
        return c

    lax.fori_loop(0, lax.shift_right_logical(qi, 1), body, 0)
    colm = lax.broadcasted_iota(jnp.int32, (tq, 128), 1)
    _, accs = _sb_tiles(qs, [km_ref[:, c] for c in cols], [vm_ref[:, c] for c in cols], tri_ref[:128, :128],
                        [carry_ref[h] for h in range(hb)], [acc_ref[h] for h in range(hb)], colm < N_META)
    for h in range(hb):
        o_ref[:, cols[h]] = accs[h].astype(o_ref.dtype)


def _attn_prompt(qkvb, kmeta, vmeta, tri, nb, seq):
    tq = ATTN_TILE
    nq = seq // tq
    hb = ATTN_HEADS_PER_STEP
    wd = hb * SB_HEAD_DIM
    ng = SB_HEADS // hb
    return pl.pallas_call(
        functools.partial(_attn_prompt_body, hb=hb),
        grid=(nb, ng, nq),
        in_specs=[
            pl.BlockSpec((tq, wd), lambda b, g, i: (b * nq + i, g)),
            pl.BlockSpec((seq, wd), lambda b, g, i: (b, ng + g)),
            pl.BlockSpec((seq, wd), lambda b, g, i: (b, 2 * ng + g)),
            pl.BlockSpec((128, wd), lambda b, g, i: (0, g)),
            pl.BlockSpec((128, wd), lambda b, g, i: (0, g)),
            pl.BlockSpec((tq, tq), lambda b, g, i: (0, 0)),
        ],
        out_specs=pl.BlockSpec((tq, wd), lambda b, g, i: (b * nq + i, g)),
        out_shape=jax.ShapeDtypeStruct((nb * seq, D_MODEL), bf16),
        scratch_shapes=[pltpu.VMEM((hb, tq, 1), f32), pltpu.VMEM((hb, tq, SB_HEAD_DIM), f32)],
        compiler_params=_cparams(("parallel", "parallel", "arbitrary"), ATTN_FLAGS),
        name="attn_prompt",
    )(qkvb, qkvb, qkvb, kmeta, vmeta, tri)


def _stack_heads(q):
    ln = q.shape[0]
    rows = SB_HEADS * ln
    qt = jnp.concatenate([q] * SB_HEADS, axis=0)
    rh = lax.shift_right_logical(lax.broadcasted_iota(jnp.int32, (rows, D_MODEL), 0), _log2(ln))
    ch = lax.shift_right_logical(lax.broadcasted_iota(jnp.int32, (rows, D_MODEL), 1), _log2(SB_HEAD_DIM))
    return jnp.where(rh == ch, qt, jnp.zeros_like(qt))


def _own_tile(q_ref, ko_ref, vo_ref, tri_ref, ln):
    rows = SB_HEADS * ln
    qbd = _stack_heads(q_ref[...])
    row = lax.broadcasted_iota(jnp.int32, (rows, 128), 0)
    col = lax.broadcasted_iota(jnp.int32, (rows, 128), 1)
    mask = col < (row & (ln - 1))
    carry = jnp.zeros((rows, 1), f32)
    acc = jnp.zeros((rows, D_MODEL), f32)
    carry, acc = _sb_tile(qbd, ko_ref[...], vo_ref[...], tri_ref[:128, :128], carry, acc, mask)
    return qbd, carry, acc


def _write_heads(o_ref, acc, ln):
    for h in range(SB_HEADS):
        cols = slice(h * SB_HEAD_DIM, (h + 1) * SB_HEAD_DIM)
        o_ref[:, cols] = acc[h * ln:(h + 1) * ln, cols].astype(o_ref.dtype)


def _attn_cached_body(q_ref, ko_ref, vo_ref, kc_hbm, vc_hbm, tri_ref, o_ref, qbd_ref, carry_ref, acc_ref,
                      kbuf_ref, vbuf_ref, sem_ref, *, ln, tk, nkt):
    b = pl.program_id(0)
    kt = pl.program_id(1)
    step = b * nkt + kt
    last = pl.num_programs(0) * nkt - 1
    slot = step & 1

    def tile_copies(bb, tt, sl):
        k0 = (nkt - 1 - tt) * tk
        cps = []
        for src, dst in ((kc_hbm, kbuf_ref), (vc_hbm, vbuf_ref)):
            for h in range(SB_HEADS):
                cps.append(pltpu.make_async_copy(
                    src.at[bb, pl.ds(k0, tk), h, :],
                    dst.at[sl, :, pl.ds(h * SB_HEAD_DIM, SB_HEAD_DIM)], sem_ref.at[sl]))
        return cps

    @pl.when(step == 0)
    def _():
        for c in tile_copies(b, kt, slot):
            c.start()

    @pl.when(step < last)
    def _():
        nxt = step + 1
        nb_ = jnp.where(kt == nkt - 1, b + 1, b)
        nt_ = jnp.where(kt == nkt - 1, 0, kt + 1)
        for c in tile_copies(nb_, nt_, nxt & 1):
            c.start()

    @pl.when(kt == 0)
    def _():
        qbd, carry, acc = _own_tile(q_ref, ko_ref, vo_ref, tri_ref, ln)
        qbd_ref[...] = qbd
        carry_ref[...] = carry
        acc_ref[...] = acc

    for c in tile_copies(b, kt, slot):
        c.wait()

    qbd = qbd_ref[...]
    subs = list(reversed(range(tk // ATTN_TILE)))

    def tile(buf_ref, s):
        return buf_ref[slot, s * ATTN_TILE:(s + 1) * ATTN_TILE, :].astype(bf16)

    carries, accs = _sb_tiles([qbd] * len(subs), [tile(kbuf_ref, s) for s in subs],
                              [tile(vbuf_ref, s) for s in subs], tri_ref[...],
                              [carry_ref[...]], [acc_ref[...]], None, group=len(subs))
    carry = carries[0]
    acc = accs[0]
    carry_ref[...] = carry
    acc_ref[...] = acc

    @pl.when(kt == pl.num_programs(1) - 1)
    def _():
        _write_heads(o_ref, acc, ln)


def _attn_cached(qkvb, k_own, v_own, k_cache, v_cache, tri, nb, ln, tk):
    past = k_cache.shape[1]
    nkt = past // tk
    rows = SB_HEADS * ln
    return pl.pallas_call(
        functools.partial(_attn_cached_body, ln=ln, tk=tk, nkt=nkt),
        grid=(nb, nkt),
        in_specs=[
            pl.BlockSpec((ln, D_MODEL), lambda b, t: (b, 0)),
            pl.BlockSpec((128, D_MODEL), lambda b, t: (b, 0)),
            pl.BlockSpec((128, D_MODEL), lambda b, t: (b, 0)),
            pl.BlockSpec(memory_space=pl.ANY),
            pl.BlockSpec(memory_space=pl.ANY),
            pl.BlockSpec((ATTN_TILE, ATTN_TILE), lambda b, t: (0, 0)),
        ],
        out_specs=pl.BlockSpec((ln, D_MODEL), lambda b, t: (b, 0)),
        out_shape=jax.ShapeDtypeStruct((nb * ln, D_MODEL), bf16),
        scratch_shapes=[
            pltpu.VMEM((rows, D_MODEL), bf16),
            pltpu.VMEM((rows, 1), f32),
            pltpu.VMEM((rows, D_MODEL), f32),
            pltpu.VMEM((2, tk, D_MODEL), f32),
            pltpu.VMEM((2, tk, D_MODEL), f32),
            pltpu.SemaphoreType.DMA((2,)),
        ],
        compiler_params=_cparams(("arbitrary", "arbitrary")),
        name="attn_cached",
    )(qkvb, k_own, v_own, k_cache, v_cache, tri)


def _attn_own_body(q_ref, ko_ref, vo_ref, tri_ref, o_ref, *, ln):
    _, _, acc = _own_tile(q_ref, ko_ref, vo_ref, tri_ref, ln)
    _write_heads(o_ref, acc, ln)


def _attn_own(q, k_own, v_own, tri, ln):
    full = lambda i: (0, 0)
    return pl.pallas_call(
        functools.partial(_attn_own_body, ln=ln),
        grid=(1,),
        in_specs=[
            pl.BlockSpec((ln, D_MODEL), full),
            pl.BlockSpec((128, D_MODEL), full),
            pl.BlockSpec((128, D_MODEL), full),
            pl.BlockSpec((ATTN_TILE, ATTN_TILE), full),
        ],
        out_specs=pl.BlockSpec((ln, D_MODEL), full),
        out_shape=jax.ShapeDtypeStruct((ln, D_MODEL), bf16),
        compiler_params=_cparams(("arbitrary",)),
        name="attn_own",
    )(q, k_own, v_own, tri)


def kernel(x_prompt, x_sample, cache_pool, state_ssm_re, state_ssm_im, cache_k, cache_v, meta_tokens, ffn_norm, ffn_w_gate, ffn_w_up, ffn_w_down, mix_norm, ab_w_in, pool_w, pool_scale, ssm_a_re, ssm_a_im, ssm_log_dt, ssm_b_re, ssm_b_im, ssm_c_re, ssm_c_im, ssm_d, ssm_w_glu, ssm_b_glu, ab_w_out, sb_w_qkv, sb_w_out, final_norm):
    nb, seq, _ = x_prompt.shape
    db, dl, _ = x_sample.shape
    assert dl == N_META == CHUNK and seq % ATTN_TILE == 0
    n_small = db + 1
    rows_s = n_small * dl
    tm_p = TM_PROMPT

    def ffn(xp, xs, which, final_gamma=None):
        xs, wg, wu, wd = _ffn(xs, ffn_norm[which], ffn_w_gate, ffn_w_up, ffn_w_down, rows_s, FFN_TILE_F_SMALL,
                              final_gamma, which)
        xp = _ffn(xp, ffn_norm[which], wg, wu, wd, TM_FFN, FFN_TILE_F, final_gamma)
        return xp, xs

    w_in = ab_w_in[0].astype(bf16)
    w_out0 = ab_w_out[0].astype(bf16)
    w_pool = pool_w[0].astype(bf16)
    w_glu = ssm_w_glu[0].astype(bf16)
    w_qkv = sb_w_qkv[0].astype(bf16)
    w_out1 = sb_w_out[0].astype(bf16)
    mats = _ssm_matrices(ssm_a_re[0], ssm_a_im[0], ssm_log_dt[0], ssm_b_re[0], ssm_b_im[0],
                         ssm_c_re[0], ssm_c_im[0])
    ti = jnp.arange(ATTN_TILE)
    tri = (ti[:, None] > ti[None, :]).astype(bf16)

    xp = x_prompt.reshape(nb * seq, D_MODEL)
    xs = jnp.concatenate([x_sample.reshape(db * dl, D_MODEL), meta_tokens.astype(f32)], axis=0)

    xp, xs = ffn(xp, xs, (0, 0))
    up = _norm_proj(xp, mix_norm[0], w_in, tm_p, D_MODEL, False)
    us = _norm_proj(xs, mix_norm[0], w_in, rows_s, D_MODEL, False)

    pad_s = -(-n_small // 8) * 8
    zero_state = jnp.zeros((1, SSM_GROUPS, SSM_STATE), f32)
    h0r = _state_to_slices(jnp.concatenate([state_ssm_re[0], zero_state], axis=0), pad_s)
    h0i = _state_to_slices(jnp.concatenate([state_ssm_im[0], zero_state], axis=0), pad_s)
    us_pad = jnp.pad(us, ((0, (pad_s - n_small) * dl), (0, 0)))
    ys, hsr, hsi = _ssm_scan(us_pad, h0r, h0i, mats, 1, pad_s, 1)
    ys = ys[:rows_s]
    hsr = hsr[:, 0]
    hsi = hsi[:, 0]
    us3 = us.reshape(n_small, dl, D_MODEL)
    hist_s = jnp.concatenate([
        jnp.concatenate([jnp.zeros((db, 1, POOL_WIDTH), f32), cache_pool[0].astype(f32)], axis=1),
        jnp.zeros((1, N_META, POOL_WIDTH), f32)], axis=0)
    p0_s = jnp.concatenate([jnp.full((db, 1, POOL_GROUP), float(N_META), f32),
                            jnp.zeros((1, 1, POOL_GROUP), f32)], axis=0)
    xs = _mix0(us, ys, xs, hist_s, p0_s, w_pool, pool_scale[0], ssm_d[0], w_glu, ssm_b_glu[0], w_out0,
               n_small, dl)

    meta_r = hsr[:, db].reshape(SLICES, 1, 1, SLICE_STATE)
    meta_i = hsi[:, db].reshape(SLICES, 1, 1, SLICE_STATE)
    yp, hpr, hpi = _ssm_scan(up, meta_r, meta_i, mats, nb, 1, seq // CHUNK)
    sl_p = MIX0_SEG
    up4 = up.reshape(nb, seq // sl_p, sl_p, D_MODEL)
    meta_tail = jnp.broadcast_to(us3[db:, :, :POOL_WIDTH], (nb, N_META, POOL_WIDTH))
    hist_p = jnp.concatenate([meta_tail[:, None], up4[:, :-1, sl_p - N_META:, :POOL_WIDTH]], axis=1)
    hist_p = hist_p.reshape(nb * (seq // sl_p), N_META, POOL_WIDTH)
    p0_p = jnp.full((nb * (seq // sl_p), 1, POOL_GROUP), float(N_META), f32)
    xp = _mix0(up, yp, xp, hist_p, p0_p, w_pool, pool_scale[0], ssm_d[0], w_glu, ssm_b_glu[0], w_out0, 1, sl_p)

    xp, xs = ffn(xp, xs, (0, 1))

    xp, xs = ffn(xp, xs, (1, 0))
    qkv_s, qkvb_s = _norm_proj(xs, mix_norm[1], w_qkv, rows_s, 1024, True)
    qkvb_p, kp, vp = _qkv_prompt(xp, mix_norm[1], w_qkv, qkv_s[db * dl:], nb, seq)

    kcol = slice(D_MODEL, 2 * D_MODEL)
    vcol = slice(2 * D_MODEL, 3 * D_MODEL)
    pad_keys = lambda a: jnp.pad(a, ((0, 0), (0, 128 - dl), (0, 0))).reshape(-1, D_MODEL)
    kb3 = qkvb_s[:, kcol].reshape(n_small, dl, D_MODEL)
    vb3 = qkvb_s[:, vcol].reshape(n_small, dl, D_MODEL)
    k_own = pad_keys(kb3[:db])
    v_own = pad_keys(vb3[:db])
    k_meta = pad_keys(kb3[db:])
    v_meta = pad_keys(vb3[db:])

    o_sample = _attn_cached(qkvb_s, k_own, v_own, cache_k[0], cache_v[0], tri, db, dl, CACHE_TILE)
    o_meta = _attn_own(qkvb_s[db * dl:, :D_MODEL], k_meta, v_meta, tri, dl)
    o_prompt = _attn_prompt(qkvb_p, k_meta, v_meta, tri, nb, seq)
    xp = _proj_res(o_prompt, w_out1, xp, tm_p)
    xs = _proj_res(jnp.concatenate([o_sample, o_meta], axis=0), w_out1, xs, rows_s)

    yp_out, ys_out = ffn(xp, xs, (1, 1), final_norm)

    y_prompt = yp_out.reshape(nb, seq, D_MODEL)
    y_sample = ys_out[:db * dl].reshape(db, dl, D_MODEL)
    up3 = up.reshape(nb, seq, D_MODEL)
    pool_p = up3[:, seq - POOL_HIST:, :POOL_WIDTH][None]
    pool_s = us3[:db, dl - POOL_HIST:, :POOL_WIDTH][None]
    re_p = _state_from_slices(hpr[:, :, 0])[None]
    im_p = _state_from_slices(hpi[:, :, 0])[None]
    re_s = _state_from_slices(hsr[:, :db])[None]
    im_s = _state_from_slices(hsi[:, :db])[None]
    heads = lambda a, n: a.reshape(n, -1, SB_HEADS, SB_HEAD_DIM)
    k3 = qkv_s[:, kcol].reshape(n_small, dl, D_MODEL)
    v3 = qkv_s[:, vcol].reshape(n_small, dl, D_MODEL)
    k_p = kp[None]
    v_p = vp[None]
    k_s = heads(k3[:db], db)[None]
    v_s = heads(v3[:db], db)[None]
    return (y_prompt, y_sample, pool_p, pool_s, re_p, im_p, re_s, im_s, k_p, v_p, k_s, v_s)
```

```python
import functools
import math

import jax
import jax.numpy as jnp
from jax import lax
from jax.experimental import pallas as pl
from jax.experimental.pallas import tpu as pltpu

f32 = jnp.float32
bf16 = jnp.bfloat16

D_MODEL = 2048
N_META = 16
RMS_EPS = 1e-6
FFN_RESIDUAL = 0.5
FFN_DIM = 5632
POOL_WIDTH = 1024
POOL_WINDOWS = (2, 4, 8, 16)
POOL_GROUP = 256
POOL_HIST = 15
SSM_WIDTH = 1024
SSM_GROUP = 16
SSM_GROUPS = 64
SSM_STATE = 64
SB_HEADS = 16
SB_HEAD_DIM = 128
LOG2E = 1.4426950408889634
SB_SCALE_LOG2 = LOG2E / math.sqrt(SB_HEAD_DIM)

LANES = 128
CHUNK = 16
SLICES = SSM_WIDTH // LANES
GPS = LANES // SSM_GROUP
SLICE_IN = CHUNK * LANES
SLICE_STATE = GPS * SSM_STATE

VMEM_LIMIT = 56 * 1024 * 1024
ATTN_TILE = 256
ATTN_HEADS_PER_STEP = 8
ATTN_FLAGS = None
FFN_TILE_F = 512
FFN_TILE_F_SMALL = 256
TM_PROMPT = 512
TM_FFN = 1024
MIX0_SEG = 256
CACHE_TILE = 1024


def _cparams(sem, flags=None):
    return pltpu.CompilerParams(dimension_semantics=sem, vmem_limit_bytes=VMEM_LIMIT, flags=flags)


def _log2(n):
    assert n & (n - 1) == 0
    return n.bit_length() - 1


def _rms(xf, g):
    ms = jnp.mean(xf * xf, axis=-1, keepdims=True)
    return xf * lax.rsqrt(ms + RMS_EPS) * g


def _ffn_body(*refs, final, cast):
    refs = list(refs)
    x_ref, g_ref, wg_ref, wu_ref, wd_ref = refs[:5]
    gf_ref = refs[5] if final else None
    o_ref = refs[5 + final]
    h_ref = refs[-1]
    j = pl.program_id(1)

    @pl.when(j == 0)
    def _():
        xf = x_ref[...]
        h_ref[...] = _rms(xf, g_ref[...]).astype(bf16)
        o_ref[...] = xf

    wg, wu, wd = wg_ref[...], wu_ref[...], wd_ref[...]
    if cast:
        wgb_ref, wub_ref, wdb_ref = refs[6 + final:9 + final]
        wg, wu, wd = wg.astype(bf16), wu.astype(bf16), wd.astype(bf16)
        wgb_ref[...] = wg
        wub_ref[...] = wu
        wdb_ref[...] = wd

    h = h_ref[...]
    g = jnp.dot(h, wg, preferred_element_type=f32)
    u = jnp.dot(h, wu, preferred_element_type=f32)
    a = (jax.nn.silu(g) * u * FFN_RESIDUAL).astype(bf16)
    o_ref[...] += jnp.dot(a, wd, preferred_element_type=f32)

    if final:
        @pl.when(j == pl.num_programs(1) - 1)
        def _():
            o_ref[...] = _rms(o_ref[...], gf_ref[...])


def _ffn(x, gamma, wg, wu, wd, tm, tf, final_gamma=None, which=None):
    rows = x.shape[0]
    final = final_gamma is not None
    cast = which is not None
    if cast:
        assert rows == tm
        la, po = which
        w_specs = [
            pl.BlockSpec((None, None, D_MODEL, tf), lambda i, j: (la, po, 0, j)),
            pl.BlockSpec((None, None, D_MODEL, tf), lambda i, j: (la, po, 0, j)),
            pl.BlockSpec((None, None, tf, D_MODEL), lambda i, j: (la, po, j, 0)),
        ]
    else:
        w_specs = [
            pl.BlockSpec((D_MODEL, tf), lambda i, j: (0, j)),
            pl.BlockSpec((D_MODEL, tf), lambda i, j: (0, j)),
            pl.BlockSpec((tf, D_MODEL), lambda i, j: (j, 0)),
        ]
    in_specs = [pl.BlockSpec((tm, D_MODEL), lambda i, j: (i, 0)), pl.BlockSpec((1, D_MODEL), lambda i, j: (0, 0))]
    in_specs += w_specs
    args = [x, gamma.reshape(1, D_MODEL), wg, wu, wd]
    if final:
        in_specs.append(pl.BlockSpec((1, D_MODEL), lambda i, j: (0, 0)))
        args.append(final_gamma.reshape(1, D_MODEL))
    out_specs = [pl.BlockSpec((tm, D_MODEL), lambda i, j: (i, 0))]
    out_shape = [jax.ShapeDtypeStruct((rows, D_MODEL), f32)]
    if cast:
        out_specs += [
            pl.BlockSpec((D_MODEL, tf), lambda i, j: (0, j)),
            pl.BlockSpec((D_MODEL, tf), lambda i, j: (0, j)),
            pl.BlockSpec((tf, D_MODEL), lambda i, j: (j, 0)),
        ]
        out_shape += [
            jax.ShapeDtypeStruct((D_MODEL, FFN_DIM), bf16),
            jax.ShapeDtypeStruct((D_MODEL, FFN_DIM), bf16),
            jax.ShapeDtypeStruct((FFN_DIM, D_MODEL), bf16),
        ]
    res = pl.pallas_call(
        functools.partial(_ffn_body, final=final, cast=cast),
        grid=(rows // tm, FFN_DIM // tf),
        in_specs=in_specs,
        out_specs=out_specs,
        out_shape=out_shape,
        scratch_shapes=[pltpu.VMEM((tm, D_MODEL), bf16)],
        compiler_params=_cparams(("arbitrary", "arbitrary")),
        name=("ffn_final" if final else "ffn") + ("_cast" if cast else ""),
    )(*args)
    return res if cast else res[0]


def _norm_proj_body(x_ref, g_ref, w_ref, *refs, with_bf16):
    if with_bf16:
        o_ref, ob_ref, h_ref = refs
    else:
        o_ref, h_ref = refs

    @pl.when(pl.program_id(1) == 0)
    def _():
        h_ref[...] = _rms(x_ref[...], g_ref[...]).astype(bf16)

    r = jnp.dot(h_ref[...], w_ref[...], preferred_element_type=f32)
    o_ref[...] = r
    if with_bf16:
        ob_ref[...] = r.astype(bf16)


def _norm_proj(x, gamma, w, tm, tn, with_bf16):
    rows = x.shape[0]
    n = w.shape[1]
    out_shape = [jax.ShapeDtypeStruct((rows, n), f32)]
    out_specs = [pl.BlockSpec((tm, tn), lambda i, j: (i, j))]
    if with_bf16:
        out_shape.append(jax.ShapeDtypeStruct((rows, n), bf16))
        out_specs.append(pl.BlockSpec((tm, tn), lambda i, j: (i, j)))
    res = pl.pallas_call(
        functools.partial(_norm_proj_body, with_bf16=with_bf16),
        grid=(rows // tm, n // tn),
        in_specs=[
            pl.BlockSpec((tm, D_MODEL), lambda i, j: (i, 0)),
            pl.BlockSpec((1, D_MODEL), lambda i, j: (0, 0)),
            pl.BlockSpec((D_MODEL, tn), lambda i, j: (0, j)),
        ],
        out_specs=out_specs,
        out_shape=out_shape,
        scratch_shapes=[pltpu.VMEM((tm, D_MODEL), bf16)],
        compiler_params=_cparams(("parallel", "arbitrary")),
        name="norm_proj_qkv" if with_bf16 else "norm_proj",
    )(x, gamma.reshape(1, D_MODEL), w)
    return res if with_bf16 else res[0]


def _qkv_prompt_body(x_ref, g_ref, w_ref, mkv_ref, qkvb_ref, kp_ref, vp_ref, h_ref, stage_ref, sem_ref, msem_ref,
                     *, tm, tn, tpb):
    i = pl.program_id(0)
    j = pl.program_id(1)
    last_i = pl.num_programs(0) - 1
    nj = 3 * D_MODEL // tn
    njq = D_MODEL // tn
    hpt = tn // SB_HEAD_DIM

    @pl.when(j == 0)
    def _():
        h_ref[...] = _rms(x_ref[...], g_ref[...]).astype(bf16)

    r = jnp.dot(h_ref[...], w_ref[:, pl.ds(pl.multiple_of(j * tn, tn), tn)], preferred_element_type=f32)
    qkvb_ref[...] = r.astype(bf16)

    b = lax.shift_right_logical(i, _log2(tpb))
    il = i & (tpb - 1)
    t0 = N_META + il * tm
    n = i * (nj - njq) + (j - njq)
    slot = n & 1

    def copies(dst_ref, sl, head0):
        return [pltpu.make_async_copy(stage_ref.at[sl, :, pl.ds(hh * SB_HEAD_DIM, SB_HEAD_DIM)],
                                      dst_ref.at[b, pl.ds(t0, tm), head0 + hh, :], sem_ref.at[sl])
                for hh in range(hpt)]

    def meta_copies(dst_ref, head0):
        return [pltpu.make_async_copy(mkv_ref.at[:, pl.ds(hh * SB_HEAD_DIM, SB_HEAD_DIM)],
                                      dst_ref.at[b, pl.ds(0, N_META), head0 + hh, :], msem_ref)
                for hh in range(hpt)]

    def send(dst_ref, head0):
        for c in copies(dst_ref, slot, head0):
            c.start()

        @pl.when(il == 0)
        def _():
            cs = meta_copies(dst_ref, head0)
            for c in cs:
                c.start()
            for c in cs:
                c.wait()

    @pl.when(j >= njq)
    def _():
        @pl.when(n >= 2)
        def _():
            for c in copies(kp_ref, slot, 0):
                c.wait()

        stage_ref[slot] = r

        @pl.when(j < 2 * njq)
        def _():
            send(kp_ref, (j - njq) * hpt)

        @pl.when(j >= 2 * njq)
        def _():
            send(vp_ref, (j - 2 * njq) * hpt)

        @pl.when(jnp.logical_and(i == last_i, j == nj - 1))
        def _():
            for c in copies(kp_ref, slot, 0) + copies(kp_ref, 1 - slot, 0):
                c.wait()


def _qkv_prompt(x, gamma, w, meta_kv, nb, seq):
    tm, tn = TM_PROMPT, D_MODEL
    tpb = seq // tm
    rows = nb * seq
    kv_shape = jax.ShapeDtypeStruct((nb, N_META + seq, SB_HEADS, SB_HEAD_DIM), f32)
    return pl.pallas_call(
        functools.partial(_qkv_prompt_body, tm=tm, tn=tn, tpb=tpb),
        grid=(rows // tm, 3 * D_MODEL // tn),
        in_specs=[
            pl.BlockSpec((tm, D_MODEL), lambda i, j: (i, 0)),
            pl.BlockSpec((1, D_MODEL), lambda i, j: (0, 0)),
            pl.BlockSpec((D_MODEL, 3 * D_MODEL), lambda i, j: (0, 0), pipeline_mode=pl.Buffered(1)),
            pl.BlockSpec((N_META, tn), lambda i, j: (0, j)),
        ],
        out_specs=[
            pl.BlockSpec((tm, tn), lambda i, j: (i, j)),
            pl.BlockSpec(memory_space=pl.ANY),
            pl.BlockSpec(memory_space=pl.ANY),
        ],
        out_shape=[jax.ShapeDtypeStruct((rows, 3 * D_MODEL), bf16), kv_shape, kv_shape],
        scratch_shapes=[
            pltpu.VMEM((tm, D_MODEL), bf16),
            pltpu.VMEM((2, tm, tn), f32),
            pltpu.SemaphoreType.DMA((2,)),
            pltpu.SemaphoreType.DMA(()),
        ],
        compiler_params=_cparams(("arbitrary", "arbitrary")),
        name="qkv_prompt",
    )(x, gamma.reshape(1, D_MODEL), w, meta_kv)


def _proj_res_body(a_ref, w_ref, x_ref, o_ref):
    o_ref[...] = x_ref[...] + jnp.dot(a_ref[...], w_ref[...], preferred_element_type=f32)


def _proj_res(a, w, x, tm):
    rows = x.shape[0]
    return pl.pallas_call(
        _proj_res_body,
        grid=(rows // tm,),
        in_specs=[
            pl.BlockSpec((tm, D_MODEL), lambda i: (i, 0)),
            pl.BlockSpec((D_MODEL, D_MODEL), lambda i: (0, 0)),
            pl.BlockSpec((tm, D_MODEL), lambda i: (i, 0)),
        ],
        out_specs=pl.BlockSpec((tm, D_MODEL), lambda i: (i, 0)),
        out_shape=jax.ShapeDtypeStruct((rows, D_MODEL), f32),
        compiler_params=_cparams(("parallel",)),
        name="proj_res",
    )(a, w, x)


def _ssm_body(u_ref, h0r_ref, h0i_ref, m_ref, er_ref, ei_ref, fr_ref, fi_ref, lr_ref, li_ref,
              y_ref, hTr_ref, hTi_ref, dr_ref, di_ref, sr_ref, si_ref, *, ns, nch):
    nc = ns * nch
    lhs = jnp.concatenate([u_ref[pl.ds(s, nc, stride=CHUNK), :].astype(bf16) for s in range(CHUNK)], axis=1)
    dr_ref[...] = jnp.dot(lhs, er_ref[0], preferred_element_type=f32)
    di_ref[...] = jnp.dot(lhs, ei_ref[0], preferred_element_type=f32)
    lr = lr_ref[0]
    li = li_ref[0]
    h0r = h0r_ref[0, 0]
    h0i = h0i_ref[0, 0]
    if nch == 1:
        sr_ref[...] = h0r
        si_ref[...] = h0i
        hr = lr * h0r - li * h0i + dr_ref[...]
        hi = lr * h0i + li * h0r + di_ref[...]
    else:
        assert ns == 1

        def step(c, carry):
            hr, hi = carry
            sr_ref[pl.ds(c, 1), :] = hr
            si_ref[pl.ds(c, 1), :] = hi
            return (lr * hr - li * hi + dr_ref[pl.ds(c, 1), :], lr * hi + li * hr + di_ref[pl.ds(c, 1), :])

        hr, hi = lax.fori_loop(0, nch, step, (h0r, h0i), unroll=8)
    hTr_ref[0, 0] = hr
    hTi_ref[0, 0] = hi

    y = (jnp.dot(lhs, m_ref[0], preferred_element_type=f32)
         + jnp.dot(sr_ref[...].astype(bf16), fr_ref[0], preferred_element_type=f32)
         + jnp.dot(si_ref[...].astype(bf16), fi_ref[0], preferred_element_type=f32))
    for t in range(CHUNK):
        y_ref[pl.ds(t, nc, stride=CHUNK), :] = y[:, t * LANES:(t + 1) * LANES]


def _ssm_scan(u, h0r, h0i, mats, nb, ns, nch):
    m, er, ei, fr, fi, lr, li = mats
    rows_b = ns * nch * CHUNK
    nc = ns * nch
    w3 = lambda j, b: (j, 0, 0)
    h4 = lambda j, b: (j, 0, 0, 0)
    return pl.pallas_call(
        functools.partial(_ssm_body, ns=ns, nch=nch),
        grid=(SLICES, nb),
        in_specs=[
            pl.BlockSpec((rows_b, LANES), lambda j, b: (b, SLICES + j)),
            pl.BlockSpec((1, 1, ns, SLICE_STATE), h4),
            pl.BlockSpec((1, 1, ns, SLICE_STATE), h4),
            pl.BlockSpec((1, SLICE_IN, SLICE_IN), w3),
            pl.BlockSpec((1, SLICE_IN, SLICE_STATE), w3),
            pl.BlockSpec((1, SLICE_IN, SLICE_STATE), w3),
            pl.BlockSpec((1, SLICE_STATE, SLICE_IN), w3),
            pl.BlockSpec((1, SLICE_STATE, SLICE_IN), w3),
            pl.BlockSpec((1, 1, SLICE_STATE), w3),
            pl.BlockSpec((1, 1, SLICE_STATE), w3),
        ],
        out_specs=[
            pl.BlockSpec((rows_b, LANES), lambda j, b: (b, j)),
            pl.BlockSpec((1, 1, ns, SLICE_STATE), lambda j, b: (j, b, 0, 0)),
            pl.BlockSpec((1, 1, ns, SLICE_STATE), lambda j, b: (j, b, 0, 0)),
        ],
        out_shape=[
            jax.ShapeDtypeStruct((nb * rows_b, SSM_WIDTH), f32),
            jax.ShapeDtypeStruct((SLICES, nb, ns, SLICE_STATE), f32),
            jax.ShapeDtypeStruct((SLICES, nb, ns, SLICE_STATE), f32),
        ],
        scratch_shapes=[pltpu.VMEM((nc, SLICE_STATE), f32)] * 4,
        compiler_params=_cparams(("arbitrary", "arbitrary")),
        name="ssm_scan",
    )(u, h0r, h0i, m, er, ei, fr, fi, lr, li)


def _ssm_matrices(a_re, a_im, log_dt, b_re, b_im, c_re, c_im):
    hp = lax.Precision.HIGHEST
    N, P, T = SSM_STATE, SSM_GROUP, CHUNK
    a_re = a_re.astype(f32)
    a_im = a_im.astype(f32)
    dt = jnp.exp(log_dt.astype(f32))[:, None]
    mag = jnp.exp(a_re * dt)
    lb_re = mag * jnp.cos(a_im * dt)
    lb_im = mag * jnp.sin(a_im * dt)
    den = a_re * a_re + a_im * a_im
    k_re = ((lb_re - 1.0) * a_re + lb_im * a_im) / den
    k_im = (lb_im * a_re - (lb_re - 1.0) * a_im) / den
    b_re = b_re.astype(f32)
    b_im = b_im.astype(f32)
    bb_re = k_re[..., None] * b_re - k_im[..., None] * b_im
    bb_im = k_re[..., None] * b_im + k_im[..., None] * b_re
    cr = c_re.astype(f32)
    ci = c_im.astype(f32)

    def pw_step(c, _):
        re, im = c
        return (re * lb_re - im * lb_im, re * lb_im + im * lb_re), (re, im)

    _, (pw_re, pw_im) = lax.scan(pw_step, (jnp.ones_like(lb_re), jnp.zeros_like(lb_re)), None, length=T + 1)

    def block_diag(x, rg):
        rows, w = x.shape[-2:]
        rep = jnp.tile(jnp.eye(w, dtype=f32), (1, GPS))
        same = (jnp.arange(rows)[:, None] // rg) == (jnp.arange(GPS * w)[None, :] // w)
        return jnp.where(same, jnp.matmul(x, rep, precision=hp), 0.0)

    x_re = cr[:, :, :, None] * bb_re[:, None] - ci[:, :, :, None] * bb_im[:, None]
    x_im = cr[:, :, :, None] * bb_im[:, None] + ci[:, :, :, None] * bb_re[:, None]
    kern = (jnp.einsum('tgn,gqnp->tgqp', pw_re[:T], x_re, precision=hp)
            - jnp.einsum('tgn,gqnp->tgqp', pw_im[:T], x_im, precision=hp))
    kern = kern.reshape(T, SLICES, GPS, P, P).transpose(1, 0, 2, 4, 3).reshape(SLICES, T, LANES, P)
    m = _toeplitz_expand(block_diag(kern, P).astype(bf16))

    rev_re = pw_re[:T][::-1]
    rev_im = pw_im[:T][::-1]
    e_re = rev_re[:, :, :, None] * bb_re[None] - rev_im[:, :, :, None] * bb_im[None]
    e_im = rev_re[:, :, :, None] * bb_im[None] + rev_im[:, :, :, None] * bb_re[None]

    def slice_e(e):
        e = e.reshape(T, SLICES, GPS, N, P).transpose(1, 0, 2, 4, 3).reshape(SLICES, T, LANES, N)
        return block_diag(e, P).astype(bf16).reshape(SLICES, SLICE_IN, SLICE_STATE)

    nx_re = pw_re[1:]
    nx_im = pw_im[1:]
    crt = cr.transpose(0, 2, 1)
    cit = ci.transpose(0, 2, 1)
    f_re = crt[None] * nx_re[:, :, :, None] - cit[None] * nx_im[:, :, :, None]
    f_im = -(crt[None] * nx_im[:, :, :, None] + cit[None] * nx_re[:, :, :, None])

    def slice_f(f):
        f = f.reshape(T, SLICES, SLICE_STATE, P).transpose(1, 0, 2, 3)
        f = block_diag(f, N).astype(bf16)
        return f.transpose(0, 2, 1, 3).reshape(SLICES, SLICE_STATE, SLICE_IN)

    lam_re = pw_re[T].reshape(SLICES, 1, SLICE_STATE)
    lam_im = pw_im[T].reshape(SLICES, 1, SLICE_STATE)
    return m, slice_e(e_re), slice_e(e_im), slice_f(f_re), slice_f(f_im), lam_re, lam_im


def _toeplitz_body(bd_ref, o_ref):
    s = pl.program_id(1)
    for t in range(CHUNK):
        blk = bd_ref[0, jnp.maximum(t - s, 0)]
        o_ref[0, :, t * LANES:(t + 1) * LANES] = jnp.where(t >= s, blk, jnp.zeros_like(blk))


def _toeplitz_expand(bd):
    return pl.pallas_call(
        _toeplitz_body,
        grid=(SLICES, CHUNK),
        in_specs=[pl.BlockSpec((1, CHUNK, LANES, LANES), lambda j, s: (j, 0, 0, 0))],
        out_specs=pl.BlockSpec((1, LANES, SLICE_IN), lambda j, s: (j, s, 0)),
        out_shape=jax.ShapeDtypeStruct((SLICES, SLICE_IN, SLICE_IN), bf16),
        compiler_params=_cparams(("parallel", "arbitrary")),
        name="toeplitz_expand",
    )(bd)


def _state_to_slices(h, pad_streams):
    n = h.shape[0]
    t = h.astype(f32).reshape(n, SLICES, SLICE_STATE).transpose(1, 0, 2)
    return jnp.pad(t, ((0, 0), (0, pad_streams - n), (0, 0)))[:, None]


def _state_from_slices(h):
    return h.transpose(1, 0, 2).reshape(h.shape[1], SSM_GROUPS, SSM_STATE)


def _mix0_body(u_ref, y_ref, x_ref, hist_ref, p0_ref, pw_ref, ps_ref, d_ref, wglu_ref, bglu_ref, wout_ref,
               o_ref, ext_ref, *, sb, sl):
    rows = sb * sl
    ext_ref[:, :N_META, :] = hist_ref[...]
    ext_ref[:, N_META:, :] = u_ref[:, :, :POOL_WIDTH]
    step = lax.broadcasted_iota(jnp.int32, (sb, sl, POOL_GROUP), 1).astype(f32)
    seen = p0_ref[...] + step + 1.0

    acc = x_ref[...].reshape(rows, D_MODEL)
    for gi, w in enumerate(POOL_WINDOWS):
        cols = slice(gi * POOL_GROUP, (gi + 1) * POOL_GROUP)
        tot = ext_ref[:, N_META:N_META + sl, cols]
        for k in range(1, w):
            tot = tot + ext_ref[:, N_META - k:N_META - k + sl, cols]
        mean = tot / jnp.minimum(seen, float(w))
        diff = (mean - ext_ref[:, N_META:N_META + sl, cols]).reshape(rows, POOL_GROUP)
        ya = jnp.dot(diff.astype(bf16), pw_ref[gi], preferred_element_type=f32) * ps_ref[:, cols]
        acc = acc + jnp.dot(ya.astype(bf16), wout_ref[cols, :], preferred_element_type=f32)

    us = u_ref[:, :, POOL_WIDTH:].reshape(rows, SSM_WIDTH)
    y = y_ref[...].reshape(rows, SSM_WIDTH) + d_ref[...] * us
    z = jax.nn.gelu(y)
    gate = jax.nn.sigmoid(jnp.dot(z.astype(bf16), wglu_ref[...], preferred_element_type=f32) + bglu_ref[...])
    yb = z * gate
    acc = acc + jnp.dot(yb.astype(bf16), wout_ref[POOL_WIDTH:, :], preferred_element_type=f32)
    o_ref[...] = acc.reshape(sb, sl, D_MODEL)


def _mix0(u, y, x, hist, p0, pool_w, pool_scale, ssm_d, w_glu, b_glu, w_out, sb, sl):
    nseg = u.shape[0] // sl
    u3 = u.reshape(nseg, sl, D_MODEL)
    y3 = y.reshape(nseg, sl, SSM_WIDTH)
    x3 = x.reshape(nseg, sl, D_MODEL)
    seg3 = lambda i: (i, 0, 0)
    const2 = lambda i: (0, 0)
    out = pl.pallas_call(
        functools.partial(_mix0_body, sb=sb, sl=sl),
        grid=(nseg // sb,),
        in_specs=[
            pl.BlockSpec((sb, sl, D_MODEL), seg3),
            pl.BlockSpec((sb, sl, SSM_WIDTH), seg3),
            pl.BlockSpec((sb, sl, D_MODEL), seg3),
            pl.BlockSpec((sb, N_META, POOL_WIDTH), seg3),
            pl.BlockSpec((sb, 1, POOL_GROUP), seg3),
            pl.BlockSpec((len(POOL_WINDOWS), POOL_GROUP, POOL_GROUP), lambda i: (0, 0, 0)),
            pl.BlockSpec((1, POOL_WIDTH), const2),
            pl.BlockSpec((1, SSM_WIDTH), const2),
            pl.BlockSpec((SSM_WIDTH, SSM_WIDTH), const2),
            pl.BlockSpec((1, SSM_WIDTH), const2),
            pl.BlockSpec((D_MODEL, D_MODEL), const2),
        ],
        out_specs=pl.BlockSpec((sb, sl, D_MODEL), seg3),
        out_shape=jax.ShapeDtypeStruct((nseg, sl, D_MODEL), f32),
        scratch_shapes=[pltpu.VMEM((sb, N_META + sl, POOL_WIDTH), f32)],
        compiler_params=_cparams(("parallel",)),
        name="mix0",
    )(u3, y3, x3, hist, p0, pool_w, pool_scale.reshape(1, POOL_WIDTH), ssm_d.reshape(1, SSM_WIDTH),
      w_glu, b_glu.reshape(1, SSM_WIDTH), w_out)
    return out.reshape(nseg * sl, D_MODEL)


def _sb_tiles(qs, kts, vts, tri, carries, accs, mask, group=1):
    n = len(qs)
    ss = [lax.dot_general(qs[i], kts[i], (((1,), (1,)), ((), ())), preferred_element_type=f32) for i in range(n)]
    log_betas, log_1mbs, his = [], [], []
    for s in ss:
        z = s * SB_SCALE_LOG2
        neg_abs = pltpu.bitcast(pltpu.bitcast(z, jnp.uint32) | jnp.uint32(0x80000000), f32)
        t = jnp.log(1.0 + jnp.exp2(neg_abs)) * LOG2E
        log_beta = jnp.minimum(z, 0.0) - t
        log_1mb = log_beta - z
        if mask is not None:
            log_1mb = jnp.where(mask, log_1mb, 0.0)
        log_betas.append(log_beta)
        log_1mbs.append(log_1mb)
        his.append(log_1mb.astype(bf16))
    sums = [jnp.dot(his[i], tri, preferred_element_type=f32) for i in range(n)]
    tile_carries, out_carries = [], []
    for c in range(n // group):
        carry = carries[c]
        for i in range(c * group, (c + 1) * group):
            tile_carries.append(carry)
            carry = carry + sums[i][:, :1] + log_1mbs[i][:, :1]
        out_carries.append(carry)
    ws = []
    for i in range(n):
        w = jnp.exp2(log_betas[i] + (sums[i] + tile_carries[i]))
        if mask is not None:
            w = jnp.where(mask, w, 0.0)
        ws.append(w.astype(bf16))
    out_accs = []
    for c in range(n // group):
        acc = accs[c]
        for i in range(c * group, (c + 1) * group):
            acc = acc + jnp.dot(ws[i], vts[i], preferred_element_type=f32)
        out_accs.append(acc)
    return out_carries, out_accs


def _sb_tile(q, kt, vt, tri, carry, acc, mask):
    carries, accs = _sb_tiles([q], [kt], [vt], tri, [carry], [acc], mask)
    return carries[0], accs[0]


def _attn_prompt_body(q_ref, k_ref, v_ref, km_ref, vm_ref, tri_ref, o_ref, carry_ref, acc_ref, *, hb):
    tq = ATTN_TILE
    qi = pl.program_id(2)
    tri = tri_ref[...]
    cols = [slice(h * SB_HEAD_DIM, (h + 1) * SB_HEAD_DIM) for h in range(hb)]
    qs = [q_ref[:, c] for c in cols]

    def sweep(js, mask, first):
        g = len(js)
        r0s = [pl.multiple_of(j * tq, tq) for j in js]
        kts = [k_ref[pl.ds(r0, tq), c] for c in cols for r0 in r0s]
        vts = [v_ref[pl.ds(r0, tq), c] for c in cols for r0 in r0s]
        if first:
            carries = [jnp.zeros((tq, 1), f32)] * hb
            accs = [jnp.zeros((tq, SB_HEAD_DIM), f32)] * hb
        else:
            carries = [carry_ref[h] for h in range(hb)]
            accs = [acc_ref[h] for h in range(hb)]
        carries, accs = _sb_tiles([q for q in qs for _ in js], kts, vts, tri, carries, accs, mask, group=g)
        for h in range(hb):
            carry_ref[h] = carries[h]
            acc_ref[h] = accs[h]

    row = lax.broadcasted_iota(jnp.int32, (tq, tq), 0)
    col = lax.broadcasted_iota(jnp.int32, (tq, tq), 1)
    sweep([qi], col < row, True)

    odd = qi & 1

    @pl.when(odd == 1)
    def _():
        sweep([qi - 1], None, False)

    def body(it, c):
        top = qi - odd - 1 - 2 * it
        sweep([top, top - 1], None, False)
        return c

    lax.fori_loop(0, lax.shift_right_logical(qi, 1), body, 0)
    colm = lax.broadcasted_iota(jnp.int32, (tq, 128), 1)
    _, accs = _sb_tiles(qs, [km_ref[:, c] for c in cols], [vm_ref[:, c] for c in cols], tri_ref[:128, :128],
                        [carry_ref[h] for h in range(hb)], [acc_ref[h] for h in range(hb)], colm < N_META)
    for h in range(hb):
        o_ref[:, cols[h]] = accs[h].astype(o_ref.dtype)


def _attn_prompt(qkvb, kmeta, vmeta, tri, nb, seq):
    tq = ATTN_TILE
    nq = seq // tq
    hb = ATTN_HEADS_PER_STEP
    wd = hb * SB_HEAD_DIM
    ng = SB_HEADS // hb
    return pl.pallas_call(
        functools.partial(_attn_prompt_body, hb=hb),
        grid=(nb, ng, nq),
        in_specs=[
            pl.BlockSpec((tq, wd), lambda b, g, i: (b * nq + i, g)),
            pl.BlockSpec((seq, wd), lambda b, g, i: (b, ng + g)),
            pl.BlockSpec((seq, wd), lambda b, g, i: (b, 2 * ng + g)),
            pl.BlockSpec((128, wd), lambda b, g, i: (0, g)),
            pl.BlockSpec((128, wd), lambda b, g, i: (0, g)),
            pl.BlockSpec((tq, tq), lambda b, g, i: (0, 0)),
        ],
        out_specs=pl.BlockSpec((tq, wd), lambda b, g, i: (b * nq + i, g)),
        out_shape=jax.ShapeDtypeStruct((nb * seq, D_MODEL), bf16),
        scratch_shapes=[pltpu.VMEM((hb, tq, 1), f32), pltpu.VMEM((hb, tq, SB_HEAD_DIM), f32)],
        compiler_params=_cparams(("parallel", "parallel", "arbitrary"), ATTN_FLAGS),
        name="attn_prompt",
    )(qkvb, qkvb, qkvb, kmeta, vmeta, tri)


def _stack_heads(q):
    ln = q.shape[0]
    rows = SB_HEADS * ln
    qt = jnp.concatenate([q] * SB_HEADS, axis=0)
    rh = lax.shift_right_logical(lax.broadcasted_iota(jnp.int32, (rows, D_MODEL), 0), _log2(ln))
    ch = lax.shift_right_logical(lax.broadcasted_iota(jnp.int32, (rows, D_MODEL), 1), _log2(SB_HEAD_DIM))
    return jnp.where(rh == ch, qt, jnp.zeros_like(qt))


def _own_tile(q_ref, ko_ref, vo_ref, tri_ref, ln):
    rows = SB_HEADS * ln
    qbd = _stack_heads(q_ref[...])
    row = lax.broadcasted_iota(jnp.int32, (rows, 128), 0)
    col = lax.broadcasted_iota(jnp.int32, (rows, 128), 1)
    mask = col < (row & (ln - 1))
    carry = jnp.zeros((rows, 1), f32)
    acc = jnp.zeros((rows, D_MODEL), f32)
    carry, acc = _sb_tile(qbd, ko_ref[...], vo_ref[...], tri_ref[:128, :128], carry, acc, mask)
    return qbd, carry, acc


def _write_heads(o_ref, acc, ln):
    for h in range(SB_HEADS):
        cols = slice(h * SB_HEAD_DIM, (h + 1) * SB_HEAD_DIM)
        o_ref[:, cols] = acc[h * ln:(h + 1) * ln, cols].astype(o_ref.dtype)


def _attn_cached_body(q_ref, ko_ref, vo_ref, kc_hbm, vc_hbm, tri_ref, o_ref, qbd_ref, carry_ref, acc_ref,
                      kbuf_ref, vbuf_ref, sem_ref, *, ln, tk, nkt):
    b = pl.program_id(0)
    kt = pl.program_id(1)
    step = b * nkt + kt
    last = pl.num_programs(0) * nkt - 1
    slot = step & 1

    def tile_copies(bb, tt, sl):
        k0 = (nkt - 1 - tt) * tk
        cps = []
        for src, dst in ((kc_hbm, kbuf_ref), (vc_hbm, vbuf_ref)):
            for h in range(SB_HEADS):
                cps.append(pltpu.make_async_copy(
                    src.at[bb, pl.ds(k0, tk), h, :],
                    dst.at[sl, :, pl.ds(h * SB_HEAD_DIM, SB_HEAD_DIM)], sem_ref.at[sl]))
        return cps

    @pl.when(step == 0)
    def _():
        for c in tile_copies(b, kt, slot):
            c.start()

    @pl.when(step < last)
    def _():
        nxt = step + 1
        nb_ = jnp.where(kt == nkt - 1, b + 1, b)
        nt_ = jnp.where(kt == nkt - 1, 0, kt + 1)
        for c in tile_copies(nb_, nt_, nxt & 1):
            c.start()

    @pl.when(kt == 0)
    def _():
        qbd, carry, acc = _own_tile(q_ref, ko_ref, vo_ref, tri_ref, ln)
        qbd_ref[...] = qbd
        carry_ref[...] = carry
        acc_ref[...] = acc

    for c in tile_copies(b, kt, slot):
        c.wait()

    qbd = qbd_ref[...]
    subs = list(reversed(range(tk // ATTN_TILE)))

    def tile(buf_ref, s):
        return buf_ref[slot, s * ATTN_TILE:(s + 1) * ATTN_TILE, :].astype(bf16)

    carries, accs = _sb_tiles([qbd] * len(subs), [tile(kbuf_ref, s) for s in subs],
                              [tile(vbuf_ref, s) for s in subs], tri_ref[...],
                              [carry_ref[...]], [acc_ref[...]], None, group=len(subs))
    carry = carries[0]
    acc = accs[0]
    carry_ref[...] = carry
    acc_ref[...] = acc

    @pl.when(kt == pl.num_programs(1) - 1)
    def _():
        _write_heads(o_ref, acc, ln)


def _attn_cached(qkvb, k_own, v_own, k_cache, v_cache, tri, nb, ln, tk):
    past = k_cache.shape[1]
    nkt = past // tk
    rows = SB_HEADS * ln
    return pl.pallas_call(
        functools.partial(_attn_cached_body, ln=ln, tk=tk, nkt=nkt),
        grid=(nb, nkt),
        in_specs=[
            pl.BlockSpec((ln, D_MODEL), lambda b, t: (b, 0)),
            pl.BlockSpec((128, D_MODEL), lambda b, t: (b, 0)),
            pl.BlockSpec((128, D_MODEL), lambda b, t: (b, 0)),
            pl.BlockSpec(memory_space=pl.ANY),
            pl.BlockSpec(memory_space=pl.ANY),
            pl.BlockSpec((ATTN_TILE, ATTN_TILE), lambda b, t: (0, 0)),
        ],
        out_specs=pl.BlockSpec((ln, D_MODEL), lambda b, t: (b, 0)),
        out_shape=jax.ShapeDtypeStruct((nb * ln, D_MODEL), bf16),
        scratch_shapes=[
            pltpu.VMEM((rows, D_MODEL), bf16),
            pltpu.VMEM((rows, 1), f32),
            pltpu.VMEM((rows, D_MODEL), f32),
            pltpu.VMEM((2, tk, D_MODEL), f32),
            pltpu.VMEM((2, tk, D_MODEL), f32),
            pltpu.SemaphoreType.DMA((2,)),
        ],
        compiler_params=_cparams(("arbitrary", "arbitrary")),
        name="attn_cached",
    )(qkvb, k_own, v_own, k_cache, v_cache, tri)


def _attn_own_body(q_ref, ko_ref, vo_ref, tri_ref, o_ref, *, ln):
    _, _, acc = _own_tile(q_ref, ko_ref, vo_ref, tri_ref, ln)
    _write_heads(o_ref, acc, ln)


def _attn_own(q, k_own, v_own, tri, ln):
    full = lambda i: (0, 0)
    return pl.pallas_call(
        functools.partial(_attn_own_body, ln=ln),
        grid=(1,),
        in_specs=[
            pl.BlockSpec((ln, D_MODEL), full),
            pl.BlockSpec((128, D_MODEL), full),
            pl.BlockSpec((128, D_MODEL), full),
            pl.BlockSpec((ATTN_TILE, ATTN_TILE), full),
        ],
        out_specs=pl.BlockSpec((ln, D_MODEL), full),
        out_shape=jax.ShapeDtypeStruct((ln, D_MODEL), bf16),
        compiler_params=_cparams(("arbitrary",)),
        name="attn_own",
    )(q, k_own, v_own, tri)


def kernel(x_prompt, x_sample, cache_pool, state_ssm_re, state_ssm_im, cache_k, cache_v, meta_tokens, ffn_norm, ffn_w_gate, ffn_w_up, ffn_w_down, mix_norm, ab_w_in, pool_w, pool_scale, ssm_a_re, ssm_a_im, ssm_log_dt, ssm_b_re, ssm_b_im, ssm_c_re, ssm_c_im, ssm_d, ssm_w_glu, ssm_b_glu, ab_w_out, sb_w_qkv, sb_w_out, final_norm):
    nb, seq, _ = x_prompt.shape
    db, dl, _ = x_sample.shape
    assert dl == N_META == CHUNK and seq % ATTN_TILE == 0
    n_small = db + 1
    rows_s = n_small * dl
    tm_p = TM_PROMPT

    def ffn(xp, xs, which, final_gamma=None):
        xs, wg, wu, wd = _ffn(xs, ffn_norm[which], ffn_w_gate, ffn_w_up, ffn_w_down, rows_s, FFN_TILE_F_SMALL,
                              final_gamma, which)
        xp = _ffn(xp, ffn_norm[which], wg, wu, wd, TM_FFN, FFN_TILE_F, final_gamma)
        return xp, xs

    w_in = ab_w_in[0].astype(bf16)
    w_out0 = ab_w_out[0].astype(bf16)
    w_pool = pool_w[0].astype(bf16)
    w_glu = ssm_w_glu[0].astype(bf16)
    w_qkv = sb_w_qkv[0].astype(bf16)
    w_out1 = sb_w_out[0].astype(bf16)
    mats = _ssm_matrices(ssm_a_re[0], ssm_a_im[0], ssm_log_dt[0], ssm_b_re[0], ssm_b_im[0],
                         ssm_c_re[0], ssm_c_im[0])
    ti = jnp.arange(ATTN_TILE)
    tri = (ti[:, None] > ti[None, :]).astype(bf16)

    xp = x_prompt.reshape(nb * seq, D_MODEL)
    xs = jnp.concatenate([x_sample.reshape(db * dl, D_MODEL), meta_tokens.astype(f32)], axis=0)

    xp, xs = ffn(xp, xs, (0, 0))
    up = _norm_proj(xp, mix_norm[0], w_in, tm_p, D_MODEL, False)
    us = _norm_proj(xs, mix_norm[0], w_in, rows_s, D_MODEL, False)

    pad_s = -(-n_small // 8) * 8
    zero_state = jnp.zeros((1, SSM_GROUPS, SSM_STATE), f32)
    h0r = _state_to_slices(jnp.concatenate([state_ssm_re[0], zero_state], axis=0), pad_s)
    h0i = _state_to_slices(jnp.concatenate([state_ssm_im[0], zero_state], axis=0), pad_s)
    us_pad = jnp.pad(us, ((0, (pad_s - n_small) * dl), (0, 0)))
    ys, hsr, hsi = _ssm_scan(us_pad, h0r, h0i, mats, 1, pad_s, 1)
    ys = ys[:rows_s]
    hsr = hsr[:, 0]
    hsi = hsi[:, 0]
    us3 = us.reshape(n_small, dl, D_MODEL)
    hist_s = jnp.concatenate([
        jnp.concatenate([jnp.zeros((db, 1, POOL_WIDTH), f32), cache_pool[0].astype(f32)], axis=1),
        jnp.zeros((1, N_META, POOL_WIDTH), f32)], axis=0)
    p0_s = jnp.concatenate([jnp.full((db, 1, POOL_GROUP), float(N_META), f32),
                            jnp.zeros((1, 1, POOL_GROUP), f32)], axis=0)
    xs = _mix0(us, ys, xs, hist_s, p0_s, w_pool, pool_scale[0], ssm_d[0], w_glu, ssm_b_glu[0], w_out0,
               n_small, dl)

    meta_r = hsr[:, db].reshape(SLICES, 1, 1, SLICE_STATE)
    meta_i = hsi[:, db].reshape(SLICES, 1, 1, SLICE_STATE)
    yp, hpr, hpi = _ssm_scan(up, meta_r, meta_i, mats, nb, 1, seq // CHUNK)
    sl_p = MIX0_SEG
    up4 = up.reshape(nb, seq // sl_p, sl_p, D_MODEL)
    meta_tail = jnp.broadcast_to(us3[db:, :, :POOL_WIDTH], (nb, N_META, POOL_WIDTH))
    hist_p = jnp.concatenate([meta_tail[:, None], up4[:, :-1, sl_p - N_META:, :POOL_WIDTH]], axis=1)
    hist_p = hist_p.reshape(nb * (seq // sl_p), N_META, POOL_WIDTH)
    p0_p = jnp.full((nb * (seq // sl_p), 1, POOL_GROUP), float(N_META), f32)
    xp = _mix0(up, yp, xp, hist_p, p0_p, w_pool, pool_scale[0], ssm_d[0], w_glu, ssm_b_glu[0], w_out0, 1, sl_p)

    xp, xs = ffn(xp, xs, (0, 1))

    xp, xs = ffn(xp, xs, (1, 0))
    qkv_s, qkvb_s = _norm_proj(xs, mix_norm[1], w_qkv, rows_s, 1024, True)
    qkvb_p, kp, vp = _qkv_prompt(xp, mix_norm[1], w_qkv, qkv_s[db * dl:], nb, seq)

    kcol = slice(D_MODEL, 2 * D_MODEL)
    vcol = slice(2 * D_MODEL, 3 * D_MODEL)
    pad_keys = lambda a: jnp.pad(a, ((0, 0), (0, 128 - dl), (0, 0))).reshape(-1, D_MODEL)
    kb3 = qkvb_s[:, kcol].reshape(n_small, dl, D_MODEL)
    vb3 = qkvb_s[:, vcol].reshape(n_small, dl, D_MODEL)
    k_own = pad_keys(kb3[:db])
    v_own = pad_keys(vb3[:db])
    k_meta = pad_keys(kb3[db:])
    v_meta = pad_keys(vb3[db:])

    o_sample = _attn_cached(qkvb_s, k_own, v_own, cache_k[0], cache_v[0], tri, db, dl, CACHE_TILE)
    o_meta = _attn_own(qkvb_s[db * dl:, :D_MODEL], k_meta, v_meta, tri, dl)
    o_prompt = _attn_prompt(qkvb_p, k_meta, v_meta, tri, nb, seq)
    xp = _proj_res(o_prompt, w_out1, xp, tm_p)
    xs = _proj_res(jnp.concatenate([o_sample, o_meta], axis=0), w_out1, xs, rows_s)

    yp_out, ys_out = ffn(xp, xs, (1, 1), final_norm)

    y_prompt = yp_out.reshape(nb, seq, D_MODEL)
    y_sample = ys_out[:db * dl].reshape(db, dl, D_MODEL)
    up3 = up.reshape(nb, seq, D_MODEL)
    pool_p = up3[:, seq - POOL_HIST:, :POOL_WIDTH][None]
    pool_s = us3[:db, dl - POOL_HIST:, :POOL_WIDTH][None]
    re_p = _state_from_slices(hpr[:, :, 0])[None]
    im_p = _state_from_slices(hpi[:, :, 0])[None]
    re_s = _state_from_slices(hsr[:, :db])[None]
    im_s = _state_from_slices(hsi[:, :db])[None]
    heads = lambda a, n: a.reshape(n, -1, SB_HEADS, SB_HEAD_DIM)
    k3 = qkv_s[:, kcol].reshape(n_small, dl, D_MODEL)
    v3 = qkv_s[:, vcol].reshape(n_small, dl, D_MODEL)
    k_p = kp[None]
    v_p = vp[None]
    k_s = heads(k3[:db], db)[None]
    v_s = heads(v3[:db], db)[None]
    return (y_prompt, y_sample, pool_p, pool_s, re_p, im_p, re_s, im_s, k_p, v_p, k_s, v_s)
```

```python
import functools
import math

import jax
import jax.numpy as jnp
from jax import lax
from jax.experimental import pallas as pl
from jax.experimental.pallas import tpu as pltpu

f32 = jnp.float32
bf16 = jnp.bfloat16

D_MODEL = 2048
N_META = 16
RMS_EPS = 1e-6
FFN_RESIDUAL = 0.5
FFN_DIM = 5632
POOL_WIDTH = 1024
POOL_WINDOWS = (2, 4, 8, 16)
POOL_GROUP = 256
POOL_HIST = 15
SSM_WIDTH = 1024
SSM_GROUP = 16
SSM_GROUPS = 64
SSM_STATE = 64
SB_HEADS = 16
SB_HEAD_DIM = 128
LOG2E = 1.4426950408889634
SB_SCALE_LOG2 = LOG2E / math.sqrt(SB_HEAD_DIM)

LANES = 128
CHUNK = 16
SLICES = SSM_WIDTH // LANES
GPS = LANES // SSM_GROUP
SLICE_IN = CHUNK * LANES
SLICE_STATE = GPS * SSM_STATE

VMEM_LIMIT = 56 * 1024 * 1024
ATTN_TILE = 256
ATTN_HEADS_PER_STEP = 8
ATTN_FLAGS = None
FFN_TILE_F = 512
FFN_TILE_F_SMALL = 256
TM_PROMPT = 512
TM_FFN = 1024
MIX0_SEG = 256
CACHE_TILE = 1024


def _cparams(sem, flags=None):
    return pltpu.CompilerParams(dimension_semantics=sem, vmem_limit_bytes=VMEM_LIMIT, flags=flags)


def _log2(n):
    assert n & (n - 1) == 0
    return n.bit_length() - 1


def _rms(xf, g):
    ms = jnp.mean(xf * xf, axis=-1, keepdims=True)
    return xf * lax.rsqrt(ms + RMS_EPS) * g


def _ffn_body(*refs, final, cast):
    refs = list(refs)
    x_ref, g_ref, wg_ref, wu_ref, wd_ref = refs[:5]
    gf_ref = refs[5] if final else None
    o_ref = refs[5 + final]
    h_ref = refs[-1]
    j = pl.program_id(1)

    @pl.when(j == 0)
    def _():
        xf = x_ref[...]
        h_ref[...] = _rms(xf, g_ref[...]).astype(bf16)
        o_ref[...] = xf

    wg, wu, wd = wg_ref[...], wu_ref[...], wd_ref[...]
    if cast:
        wgb_ref, wub_ref, wdb_ref = refs[6 + final:9 + final]
        wg, wu, wd = wg.astype(bf16), wu.astype(bf16), wd.astype(bf16)
        wgb_ref[...] = wg
        wub_ref[...] = wu
        wdb_ref[...] = wd

    h = h_ref[...]
    g = jnp.dot(h, wg, preferred_element_type=f32)
    u = jnp.dot(h, wu, preferred_element_type=f32)
    a = (jax.nn.silu(g) * u * FFN_RESIDUAL).astype(bf16)
    o_ref[...] += jnp.dot(a, wd, preferred_element_type=f32)

    if final:
        @pl.when(j == pl.num_programs(1) - 1)
        def _():
            o_ref[...] = _rms(o_ref[...], gf_ref[...])


def _ffn(x, gamma, wg, wu, wd, tm, tf, final_gamma=None, which=None):
    rows = x.shape[0]
    final = final_gamma is not None
    cast = which is not None
    if cast:
        assert rows == tm
        la, po = which
        w_specs = [
            pl.BlockSpec((None, None, D_MODEL, tf), lambda i, j: (la, po, 0, j)),
            pl.BlockSpec((None, None, D_MODEL, tf), lambda i, j: (la, po, 0, j)),
            pl.BlockSpec((None, None, tf, D_MODEL), lambda i, j: (la, po, j, 0)),
        ]
    else:
        w_specs = [
            pl.BlockSpec((D_MODEL, tf), lambda i, j: (0, j)),
            pl.BlockSpec((D_MODEL, tf), lambda i, j: (0, j)),
            pl.BlockSpec((tf, D_MODEL), lambda i, j: (j, 0)),
        ]
    in_specs = [pl.BlockSpec((tm, D_MODEL), lambda i, j: (i, 0)), pl.BlockSpec((1, D_MODEL), lambda i, j: (0, 0))]
    in_specs += w_specs
    args = [x, gamma.reshape(1, D_MODEL), wg, wu, wd]
    if final:
        in_specs.append(pl.BlockSpec((1, D_MODEL), lambda i, j: (0, 0)))
        args.append(final_gamma.reshape(1, D_MODEL))
    out_specs = [pl.BlockSpec((tm, D_MODEL), lambda i, j: (i, 0))]
    out_shape = [jax.ShapeDtypeStruct((rows, D_MODEL), f32)]
    if cast:
        out_specs += [
            pl.BlockSpec((D_MODEL, tf), lambda i, j: (0, j)),
            pl.BlockSpec((D_MODEL, tf), lambda i, j: (0, j)),
            pl.BlockSpec((tf, D_MODEL), lambda i, j: (j, 0)),
        ]
        out_shape += [
            jax.ShapeDtypeStruct((D_MODEL, FFN_DIM), bf16),
            jax.ShapeDtypeStruct((D_MODEL, FFN_DIM), bf16),
            jax.ShapeDtypeStruct((FFN_DIM, D_MODEL), bf16),
        ]
    res = pl.pallas_call(
        functools.partial(_ffn_body, final=final, cast=cast),
        grid=(rows // tm, FFN_DIM // tf),
        in_specs=in_specs,
        out_specs=out_specs,
        out_shape=out_shape,
        scratch_shapes=[pltpu.VMEM((tm, D_MODEL), bf16)],
        compiler_params=_cparams(("arbitrary", "arbitrary")),
        name=("ffn_final" if final else "ffn") + ("_cast" if cast else ""),
    )(*args)
    return res if cast else res[0]


def _norm_proj_body(x_ref, g_ref, w_ref, *refs, with_bf16):
    if with_bf16:
        o_ref, ob_ref, h_ref = refs
    else:
        o_ref, h_ref = refs

    @pl.when(pl.program_id(1) == 0)
    def _():
        h_ref[...] = _rms(x_ref[...], g_ref[...]).astype(bf16)

    r = jnp.dot(h_ref[...], w_ref[...], preferred_element_type=f32)
    o_ref[...] = r
    if with_bf16:
        ob_ref[...] = r.astype(bf16)


def _norm_proj(x, gamma, w, tm, tn, with_bf16):
    rows = x.shape[0]
    n = w.shape[1]
    out_shape = [jax.ShapeDtypeStruct((rows, n), f32)]
    out_specs = [pl.BlockSpec((tm, tn), lambda i, j: (i, j))]
    if with_bf16:
        out_shape.append(jax.ShapeDtypeStruct((rows, n), bf16))
        out_specs.append(pl.BlockSpec((tm, tn), lambda i, j: (i, j)))
    res = pl.pallas_call(
        functools.partial(_norm_proj_body, with_bf16=with_bf16),
        grid=(rows // tm, n // tn),
        in_specs=[
            pl.BlockSpec((tm, D_MODEL), lambda i, j: (i, 0)),
            pl.BlockSpec((1, D_MODEL), lambda i, j: (0, 0)),
            pl.BlockSpec((D_MODEL, tn), lambda i, j: (0, j)),
        ],
        out_specs=out_specs,
        out_shape=out_shape,
        scratch_shapes=[pltpu.VMEM((tm, D_MODEL), bf16)],
        compiler_params=_cparams(("parallel", "arbitrary")),
        name="norm_proj_qkv" if with_bf16 else "norm_proj",
    )(x, gamma.reshape(1, D_MODEL), w)
    return res if with_bf16 else res[0]


def _qkv_prompt_body(x_ref, g_ref, w_ref, mkv_ref, qkvb_ref, kp_ref, vp_ref, h_ref, stage_ref, sem_ref, msem_ref,
                     *, tm, tn, tpb):
    i = pl.program_id(0)
    j = pl.program_id(1)
    last_i = pl.num_programs(0) - 1
    nj = 3 * D_MODEL // tn
    njq = D_MODEL // tn
    hpt = tn // SB_HEAD_DIM

    @pl.when(j == 0)
    def _():
        h_ref[...] = _rms(x_ref[...], g_ref[...]).astype(bf16)

    r = jnp.dot(h_ref[...], w_ref[:, pl.ds(pl.multiple_of(j * tn, tn), tn)], preferred_element_type=f32)
    qkvb_ref[...] = r.astype(bf16)

    b = lax.shift_right_logical(i, _log2(tpb))
    il = i & (tpb - 1)
    t0 = N_META + il * tm
    n = i * (nj - njq) + (j - njq)
    slot = n & 1

    def copies(dst_ref, sl, head0):
        return [pltpu.make_async_copy(stage_ref.at[sl, :, pl.ds(hh * SB_HEAD_DIM, SB_HEAD_DIM)],
                                      dst_ref.at[b, pl.ds(t0, tm), head0 + hh, :], sem_ref.at[sl])
                for hh in range(hpt)]

    def meta_copies(dst_ref, head0):
        return [pltpu.make_async_copy(mkv_ref.at[:, pl.ds(hh * SB_HEAD_DIM, SB_HEAD_DIM)],
                                      dst_ref.at[b, pl.ds(0, N_META), head0 + hh, :], msem_ref)
                for hh in range(hpt)]

    def send(dst_ref, head0):
        for n_c, c in enumerate(copies(dst_ref, slot, head0)):
            c.start(priority=n_c % 2)

        @pl.when(il == 0)
        def _():
            cs = meta_copies(dst_ref, head0)
            for c in cs:
                c.start()
            for c in cs:
                c.wait()

    @pl.when(j >= njq)
    def _():
        @pl.when(n >= 2)
        def _():
            for c in copies(kp_ref, slot, 0):
                c.wait()

        stage_ref[slot] = r

        @pl.when(j < 2 * njq)
        def _():
            send(kp_ref, (j - njq) * hpt)

        @pl.when(j >= 2 * njq)
        def _():
            send(vp_ref, (j - 2 * njq) * hpt)

        @pl.when(jnp.logical_and(i == last_i, j == nj - 1))
        def _():
            for c in copies(kp_ref, slot, 0) + copies(kp_ref, 1 - slot, 0):
                c.wait()


def _qkv_prompt(x, gamma, w, meta_kv, nb, seq):
    tm, tn = TM_PROMPT, D_MODEL
    tpb = seq // tm
    rows = nb * seq
    kv_shape = jax.ShapeDtypeStruct((nb, N_META + seq, SB_HEADS, SB_HEAD_DIM), f32)
    return pl.pallas_call(
        functools.partial(_qkv_prompt_body, tm=tm, tn=tn, tpb=tpb),
        grid=(rows // tm, 3 * D_MODEL // tn),
        in_specs=[
            pl.BlockSpec((tm, D_MODEL), lambda i, j: (i, 0)),
            pl.BlockSpec((1, D_MODEL), lambda i, j: (0, 0)),
            pl.BlockSpec((D_MODEL, 3 * D_MODEL), lambda i, j: (0, 0), pipeline_mode=pl.Buffered(1)),
            pl.BlockSpec((N_META, tn), lambda i, j: (0, j)),
        ],
        out_specs=[
            pl.BlockSpec((tm, tn), lambda i, j: (i, j)),
            pl.BlockSpec(memory_space=pl.ANY),
            pl.BlockSpec(memory_space=pl.ANY),
        ],
        out_shape=[jax.ShapeDtypeStruct((rows, 3 * D_MODEL), bf16), kv_shape, kv_shape],
        scratch_shapes=[
            pltpu.VMEM((tm, D_MODEL), bf16),
            pltpu.VMEM((2, tm, tn), f32),
            pltpu.SemaphoreType.DMA((2,)),
            pltpu.SemaphoreType.DMA(()),
        ],
        compiler_params=_cparams(("arbitrary", "arbitrary")),
        name="qkv_prompt",
    )(x, gamma.reshape(1, D_MODEL), w, meta_kv)


def _proj_res_body(a_ref, w_ref, x_ref, o_ref):
    o_ref[...] = x_ref[...] + jnp.dot(a_ref[...], w_ref[...], preferred_element_type=f32)


def _proj_res(a, w, x, tm):
    rows = x.shape[0]
    return pl.pallas_call(
        _proj_res_body,
        grid=(rows // tm,),
        in_specs=[
            pl.BlockSpec((tm, D_MODEL), lambda i: (i, 0)),
            pl.BlockSpec((D_MODEL, D_MODEL), lambda i: (0, 0)),
            pl.BlockSpec((tm, D_MODEL), lambda i: (i, 0)),
        ],
        out_specs=pl.BlockSpec((tm, D_MODEL), lambda i: (i, 0)),
        out_shape=jax.ShapeDtypeStruct((rows, D_MODEL), f32),
        compiler_params=_cparams(("parallel",)),
        name="proj_res",
    )(a, w, x)


def _ssm_body(u_ref, h0r_ref, h0i_ref, m_ref, er_ref, ei_ref, fr_ref, fi_ref, lr_ref, li_ref,
              y_ref, hTr_ref, hTi_ref, dr_ref, di_ref, sr_ref, si_ref, *, ns, nch):
    nc = ns * nch
    lhs = jnp.concatenate([u_ref[pl.ds(s, nc, stride=CHUNK), :].astype(bf16) for s in range(CHUNK)], axis=1)
    dr_ref[...] = jnp.dot(lhs, er_ref[0], preferred_element_type=f32)
    di_ref[...] = jnp.dot(lhs, ei_ref[0], preferred_element_type=f32)
    lr = lr_ref[0]
    li = li_ref[0]
    h0r = h0r_ref[0, 0]
    h0i = h0i_ref[0, 0]
    if nch == 1:
        sr_ref[...] = h0r
        si_ref[...] = h0i
        hr = lr * h0r - li * h0i + dr_ref[...]
        hi = lr * h0i + li * h0r + di_ref[...]
    else:
        assert ns == 1

        def step(c, carry):
            hr, hi = carry
            sr_ref[pl.ds(c, 1), :] = hr
            si_ref[pl.ds(c, 1), :] = hi
            return (lr * hr - li * hi + dr_ref[pl.ds(c, 1), :], lr * hi + li * hr + di_ref[pl.ds(c, 1), :])

        hr, hi = lax.fori_loop(0, nch, step, (h0r, h0i), unroll=8)
    hTr_ref[0, 0] = hr
    hTi_ref[0, 0] = hi

    y = (jnp.dot(lhs, m_ref[0], preferred_element_type=f32)
         + jnp.dot(sr_ref[...].astype(bf16), fr_ref[0], preferred_element_type=f32)
         + jnp.dot(si_ref[...].astype(bf16), fi_ref[0], preferred_element_type=f32))
    for t in range(CHUNK):
        y_ref[pl.ds(t, nc, stride=CHUNK), :] = y[:, t * LANES:(t + 1) * LANES]


def _ssm_scan(u, h0r, h0i, mats, nb, ns, nch):
    m, er, ei, fr, fi, lr, li = mats
    rows_b = ns * nch * CHUNK
    nc = ns * nch
    w3 = lambda j, b: (j, 0, 0)
    h4 = lambda j, b: (j, 0, 0, 0)
    return pl.pallas_call(
        functools.partial(_ssm_body, ns=ns, nch=nch),
        grid=(SLICES, nb),
        in_specs=[
            pl.BlockSpec((rows_b, LANES), lambda j, b: (b, SLICES + j)),
            pl.BlockSpec((1, 1, ns, SLICE_STATE), h4),
            pl.BlockSpec((1, 1, ns, SLICE_STATE), h4),
            pl.BlockSpec((1, SLICE_IN, SLICE_IN), w3),
            pl.BlockSpec((1, SLICE_IN, SLICE_STATE), w3),
            pl.BlockSpec((1, SLICE_IN, SLICE_STATE), w3),
            pl.BlockSpec((1, SLICE_STATE, SLICE_IN), w3),
            pl.BlockSpec((1, SLICE_STATE, SLICE_IN), w3),
            pl.BlockSpec((1, 1, SLICE_STATE), w3),
            pl.BlockSpec((1, 1, SLICE_STATE), w3),
        ],
        out_specs=[
            pl.BlockSpec((rows_b, LANES), lambda j, b: (b, j)),
            pl.BlockSpec((1, 1, ns, SLICE_STATE), lambda j, b: (j, b, 0, 0)),
            pl.BlockSpec((1, 1, ns, SLICE_STATE), lambda j, b: (j, b, 0, 0)),
        ],
        out_shape=[
            jax.ShapeDtypeStruct((nb * rows_b, SSM_WIDTH), f32),
            jax.ShapeDtypeStruct((SLICES, nb, ns, SLICE_STATE), f32),
            jax.ShapeDtypeStruct((SLICES, nb, ns, SLICE_STATE), f32),
        ],
        scratch_shapes=[pltpu.VMEM((nc, SLICE_STATE), f32)] * 4,
        compiler_params=_cparams(("arbitrary", "arbitrary")),
        name="ssm_scan",
    )(u, h0r, h0i, m, er, ei, fr, fi, lr, li)


def _ssm_matrices(a_re, a_im, log_dt, b_re, b_im, c_re, c_im):
    hp = lax.Precision.HIGHEST
    N, P, T = SSM_STATE, SSM_GROUP, CHUNK
    a_re = a_re.astype(f32)
    a_im = a_im.astype(f32)
    dt = jnp.exp(log_dt.astype(f32))[:, None]
    mag = jnp.exp(a_re * dt)
    lb_re = mag * jnp.cos(a_im * dt)
    lb_im = mag * jnp.sin(a_im * dt)
    den = a_re * a_re + a_im * a_im
    k_re = ((lb_re - 1.0) * a_re + lb_im * a_im) / den
    k_im = (lb_im * a_re - (lb_re - 1.0) * a_im) / den
    b_re = b_re.astype(f32)
    b_im = b_im.astype(f32)
    bb_re = k_re[..., None] * b_re - k_im[..., None] * b_im
    bb_im = k_re[..., None] * b_im + k_im[..., None] * b_re
    cr = c_re.astype(f32)
    ci = c_im.astype(f32)

    def pw_step(c, _):
        re, im = c
        return (re * lb_re - im * lb_im, re * lb_im + im * lb_re), (re, im)

    _, (pw_re, pw_im) = lax.scan(pw_step, (jnp.ones_like(lb_re), jnp.zeros_like(lb_re)), None, length=T + 1)

    def block_diag(x, rg):
        rows, w = x.shape[-2:]
        rep = jnp.tile(jnp.eye(w, dtype=f32), (1, GPS))
        same = (jnp.arange(rows)[:, None] // rg) == (jnp.arange(GPS * w)[None, :] // w)
        return jnp.where(same, jnp.matmul(x, rep, precision=hp), 0.0)

    x_re = cr[:, :, :, None] * bb_re[:, None] - ci[:, :, :, None] * bb_im[:, None]
    x_im = cr[:, :, :, None] * bb_im[:, None] + ci[:, :, :, None] * bb_re[:, None]
    kern = (jnp.einsum('tgn,gqnp->tgqp', pw_re[:T], x_re, precision=hp)
            - jnp.einsum('tgn,gqnp->tgqp', pw_im[:T], x_im, precision=hp))
    kern = kern.reshape(T, SLICES, GPS, P, P).transpose(1, 0, 2, 4, 3).reshape(SLICES, T, LANES, P)
    m = _toeplitz_expand(block_diag(kern, P).astype(bf16))

    rev_re = pw_re[:T][::-1]
    rev_im = pw_im[:T][::-1]
    e_re = rev_re[:, :, :, None] * bb_re[None] - rev_im[:, :, :, None] * bb_im[None]
    e_im = rev_re[:, :, :, None] * bb_im[None] + rev_im[:, :, :, None] * bb_re[None]

    def slice_e(e):
        e = e.reshape(T, SLICES, GPS, N, P).transpose(1, 0, 2, 4, 3).reshape(SLICES, T, LANES, N)
        return block_diag(e, P).astype(bf16).reshape(SLICES, SLICE_IN, SLICE_STATE)

    nx_re = pw_re[1:]
    nx_im = pw_im[1:]
    crt = cr.transpose(0, 2, 1)
    cit = ci.transpose(0, 2, 1)
    f_re = crt[None] * nx_re[:, :, :, None] - cit[None] * nx_im[:, :, :, None]
    f_im = -(crt[None] * nx_im[:, :, :, None] + cit[None] * nx_re[:, :, :, None])

    def slice_f(f):
        f = f.reshape(T, SLICES, SLICE_STATE, P).transpose(1, 0, 2, 3)
        f = block_diag(f, N).astype(bf16)
        return f.transpose(0, 2, 1, 3).reshape(SLICES, SLICE_STATE, SLICE_IN)

    lam_re = pw_re[T].reshape(SLICES, 1, SLICE_STATE)
    lam_im = pw_im[T].reshape(SLICES, 1, SLICE_STATE)
    return m, slice_e(e_re), slice_e(e_im), slice_f(f_re), slice_f(f_im), lam_re, lam_im


def _toeplitz_body(bd_ref, o_ref):
    s = pl.program_id(1)
    for t in range(CHUNK):
        blk = bd_ref[0, jnp.maximum(t - s, 0)]
        o_ref[0, :, t * LANES:(t + 1) * LANES] = jnp.where(t >= s, blk, jnp.zeros_like(blk))


def _toeplitz_expand(bd):
    return pl.pallas_call(
        _toeplitz_body,
        grid=(SLICES, CHUNK),
        in_specs=[pl.BlockSpec((1, CHUNK, LANES, LANES), lambda j, s: (j, 0, 0, 0))],
        out_specs=pl.BlockSpec((1, LANES, SLICE_IN), lambda j, s: (j, s, 0)),
        out_shape=jax.ShapeDtypeStruct((SLICES, SLICE_IN, SLICE_IN), bf16),
        compiler_params=_cparams(("parallel", "arbitrary")),
        name="toeplitz_expand",
    )(bd)


def _state_to_slices(h, pad_streams):
    n = h.shape[0]
    t = h.astype(f32).reshape(n, SLICES, SLICE_STATE).transpose(1, 0, 2)
    return jnp.pad(t, ((0, 0), (0, pad_streams - n), (0, 0)))[:, None]


def _state_from_slices(h):
    return h.transpose(1, 0, 2).reshape(h.shape[1], SSM_GROUPS, SSM_STATE)


def _mix0_body(u_ref, y_ref, x_ref, hist_ref, p0_ref, pw_ref, ps_ref, d_ref, wglu_ref, bglu_ref, wout_ref,
               o_ref, ext_ref, *, sb, sl):
    rows = sb * sl
    ext_ref[:, :N_META, :] = hist_ref[...]
    ext_ref[:, N_META:, :] = u_ref[:, :, :POOL_WIDTH]
    step = lax.broadcasted_iota(jnp.int32, (sb, sl, POOL_GROUP), 1).astype(f32)
    seen = p0_ref[...] + step + 1.0

    acc = x_ref[...].reshape(rows, D_MODEL)
    for gi, w in enumerate(POOL_WINDOWS):
        cols = slice(gi * POOL_GROUP, (gi + 1) * POOL_GROUP)
        tot = ext_ref[:, N_META:N_META + sl, cols]
        for k in range(1, w):
            tot = tot + ext_ref[:, N_META - k:N_META - k + sl, cols]
        mean = tot / jnp.minimum(seen, float(w))
        diff = (mean - ext_ref[:, N_META:N_META + sl, cols]).reshape(rows, POOL_GROUP)
        ya = jnp.dot(diff.astype(bf16), pw_ref[gi], preferred_element_type=f32) * ps_ref[:, cols]
        acc = acc + jnp.dot(ya.astype(bf16), wout_ref[cols, :], preferred_element_type=f32)

    us = u_ref[:, :, POOL_WIDTH:].reshape(rows, SSM_WIDTH)
    y = y_ref[...].reshape(rows, SSM_WIDTH) + d_ref[...] * us
    z = jax.nn.gelu(y)
    gate = jax.nn.sigmoid(jnp.dot(z.astype(bf16), wglu_ref[...], preferred_element_type=f32) + bglu_ref[...])
    yb = z * gate
    acc = acc + jnp.dot(yb.astype(bf16), wout_ref[POOL_WIDTH:, :], preferred_element_type=f32)
    o_ref[...] = acc.reshape(sb, sl, D_MODEL)


def _mix0(u, y, x, hist, p0, pool_w, pool_scale, ssm_d, w_glu, b_glu, w_out, sb, sl):
    nseg = u.shape[0] // sl
    u3 = u.reshape(nseg, sl, D_MODEL)
    y3 = y.reshape(nseg, sl, SSM_WIDTH)
    x3 = x.reshape(nseg, sl, D_MODEL)
    seg3 = lambda i: (i, 0, 0)
    const2 = lambda i: (0, 0)
    out = pl.pallas_call(
        functools.partial(_mix0_body, sb=sb, sl=sl),
        grid=(nseg // sb,),
        in_specs=[
            pl.BlockSpec((sb, sl, D_MODEL), seg3),
            pl.BlockSpec((sb, sl, SSM_WIDTH), seg3),
            pl.BlockSpec((sb, sl, D_MODEL), seg3),
            pl.BlockSpec((sb, N_META, POOL_WIDTH), seg3),
            pl.BlockSpec((sb, 1, POOL_GROUP), seg3),
            pl.BlockSpec((len(POOL_WINDOWS), POOL_GROUP, POOL_GROUP), lambda i: (0, 0, 0)),
            pl.BlockSpec((1, POOL_WIDTH), const2),
            pl.BlockSpec((1, SSM_WIDTH), const2),
            pl.BlockSpec((SSM_WIDTH, SSM_WIDTH), const2),
            pl.BlockSpec((1, SSM_WIDTH), const2),
            pl.BlockSpec((D_MODEL, D_MODEL), const2),
        ],
        out_specs=pl.BlockSpec((sb, sl, D_MODEL), seg3),
        out_shape=jax.ShapeDtypeStruct((nseg, sl, D_MODEL), f32),
        scratch_shapes=[pltpu.VMEM((sb, N_META + sl, POOL_WIDTH), f32)],
        compiler_params=_cparams(("parallel",)),
        name="mix0",
    )(u3, y3, x3, hist, p0, pool_w, pool_scale.reshape(1, POOL_WIDTH), ssm_d.reshape(1, SSM_WIDTH),
      w_glu, b_glu.reshape(1, SSM_WIDTH), w_out)
    return out.reshape(nseg * sl, D_MODEL)


def _sb_tiles(qs, kts, vts, tri, carries, accs, mask, group=1):
    n = len(qs)
    ss = [lax.dot_general(qs[i], kts[i], (((1,), (1,)), ((), ())), preferred_element_type=f32) for i in range(n)]
    log_betas, log_1mbs, his = [], [], []
    for s in ss:
        z = s * SB_SCALE_LOG2
        neg_abs = pltpu.bitcast(pltpu.bitcast(z, jnp.uint32) | jnp.uint32(0x80000000), f32)
        t = jnp.log(1.0 + jnp.exp2(neg_abs)) * LOG2E
        log_beta = jnp.minimum(z, 0.0) - t
        log_1mb = log_beta - z
        if mask is not None:
            log_1mb = jnp.where(mask, log_1mb, 0.0)
        log_betas.append(log_beta)
        log_1mbs.append(log_1mb)
        his.append(log_1mb.astype(bf16))
    sums = [jnp.dot(his[i], tri, preferred_element_type=f32) for i in range(n)]
    tile_carries, out_carries = [], []
    for c in range(n // group):
        carry = carries[c]
        for i in range(c * group, (c + 1) * group):
            tile_carries.append(carry)
            carry = carry + sums[i][:, :1] + log_1mbs[i][:, :1]
        out_carries.append(carry)
    ws = []
    for i in range(n):
        w = jnp.exp2(log_betas[i] + (sums[i] + tile_carries[i]))
        if mask is not None:
            w = jnp.where(mask, w, 0.0)
        ws.append(w.astype(bf16))
    out_accs = []
    for c in range(n // group):
        acc = accs[c]
        for i in range(c * group, (c + 1) * group):
            acc = acc + jnp.dot(ws[i], vts[i], preferred_element_type=f32)
        out_accs.append(acc)
    return out_carries, out_accs


def _sb_tile(q, kt, vt, tri, carry, acc, mask):
    carries, accs = _sb_tiles([q], [kt], [vt], tri, [carry], [acc], mask)
    return carries[0], accs[0]


def _attn_prompt_body(q_ref, k_ref, v_ref, km_ref, vm_ref, tri_ref, o_ref, carry_ref, acc_ref, *, hb):
    tq = ATTN_TILE
    qi = pl.program_id(2)
    tri = tri_ref[...]
    cols = [slice(h * SB_HEAD_DIM, (h + 1) * SB_HEAD_DIM) for h in range(hb)]
    qs = [q_ref[:, c] for c in cols]

    def sweep(js, mask, first):
        g = len(js)
        r0s = [pl.multiple_of(j * tq, tq) for j in js]
        kts = [k_ref[pl.ds(r0, tq), c] for c in cols for r0 in r0s]
        vts = [v_ref[pl.ds(r0, tq), c] for c in cols for r0 in r0s]
        if first:
            carries = [jnp.zeros((tq, 1), f32)] * hb
            accs = [jnp.zeros((tq, SB_HEAD_DIM), f32)] * hb
        else:
            carries = [carry_ref[h] for h in range(hb)]
            accs = [acc_ref[h] for h in range(hb)]
        carries, accs = _sb_tiles([q for q in qs for _ in js], kts, vts, tri, carries, accs, mask, group=g)
        for h in range(hb):
            carry_ref[h] = carries[h]
            acc_ref[h] = accs[h]

    row = lax.broadcasted_iota(jnp.int32, (tq, tq), 0)
    col = lax.broadcasted_iota(jnp.int32, (tq, tq), 1)
    sweep([qi], col < row, True)

    odd = qi & 1

    @pl.when(odd == 1)
    def _():
        sweep([qi - 1], None, False)

    def body(it, c):
        top = qi - odd - 1 - 2 * it
        sweep([top, top - 1], None, False)
        return c

    lax.fori_loop(0, lax.shift_right_logical(qi, 1), body, 0)
    colm = lax.broadcasted_iota(jnp.int32, (tq, 128), 1)
    _, accs = _sb_tiles(qs, [km_ref[:, c] for c in cols], [vm_ref[:, c] for c in cols], tri_ref[:128, :128],
                        [carry_ref[h] for h in range(hb)], [acc_ref[h] for h in range(hb)], colm < N_META)
    for h in range(hb):
        o_ref[:, cols[h]] = accs[h].astype(o_ref.dtype)


def _attn_prompt(qkvb, kmeta, vmeta, tri, nb, seq):
    tq = ATTN_TILE
    nq = seq // tq
    hb = ATTN_HEADS_PER_STEP
    wd = hb * SB_HEAD_DIM
    ng = SB_HEADS // hb
    return pl.pallas_call(
        functools.partial(_attn_prompt_body, hb=hb),
        grid=(nb, ng, nq),
        in_specs=[
            pl.BlockSpec((tq, wd), lambda b, g, i: (b * nq + i, g)),
            pl.BlockSpec((seq, wd), lambda b, g, i: (b, ng + g)),
            pl.BlockSpec((seq, wd), lambda b, g, i: (b, 2 * ng + g)),
            pl.BlockSpec((128, wd), lambda b, g, i: (0, g)),
            pl.BlockSpec((128, wd), lambda b, g, i: (0, g)),
            pl.BlockSpec((tq, tq), lambda b, g, i: (0, 0)),
        ],
        out_specs=pl.BlockSpec((tq, wd), lambda b, g, i: (b * nq + i, g)),
        out_shape=jax.ShapeDtypeStruct((nb * seq, D_MODEL), bf16),
        scratch_shapes=[pltpu.VMEM((hb, tq, 1), f32), pltpu.VMEM((hb, tq, SB_HEAD_DIM), f32)],
        compiler_params=_cparams(("parallel", "parallel", "arbitrary"), ATTN_FLAGS),
        name="attn_prompt",
    )(qkvb, qkvb, qkvb, kmeta, vmeta, tri)


def _stack_heads(q):
    ln = q.shape[0]
    rows = SB_HEADS * ln
    qt = jnp.concatenate([q] * SB_HEADS, axis=0)
    rh = lax.shift_right_logical(lax.broadcasted_iota(jnp.int32, (rows, D_MODEL), 0), _log2(ln))
    ch = lax.shift_right_logical(lax.broadcasted_iota(jnp.int32, (rows, D_MODEL), 1), _log2(SB_HEAD_DIM))
    return jnp.where(rh == ch, qt, jnp.zeros_like(qt))


def _own_tile(q_ref, ko_ref, vo_ref, tri_ref, ln):
    rows = SB_HEADS * ln
    qbd = _stack_heads(q_ref[...])
    row = lax.broadcasted_iota(jnp.int32, (rows, 128), 0)
    col = lax.broadcasted_iota(jnp.int32, (rows, 128), 1)
    mask = col < (row & (ln - 1))
    carry = jnp.zeros((rows, 1), f32)
    acc = jnp.zeros((rows, D_MODEL), f32)
    carry, acc = _sb_tile(qbd, ko_ref[...], vo_ref[...], tri_ref[:128, :128], carry, acc, mask)
    return qbd, carry, acc


def _write_heads(o_ref, acc, ln):
    for h in range(SB_HEADS):
        cols = slice(h * SB_HEAD_DIM, (h + 1) * SB_HEAD_DIM)
        o_ref[:, cols] = acc[h * ln:(h + 1) * ln, cols].astype(o_ref.dtype)


def _attn_cached_body(q_ref, ko_ref, vo_ref, kc_hbm, vc_hbm, tri_ref, o_ref, qbd_ref, carry_ref, acc_ref,
                      kbuf_ref, vbuf_ref, sem_ref, *, ln, tk, nkt):
    b = pl.program_id(0)
    kt = pl.program_id(1)
    step = b * nkt + kt
    last = pl.num_programs(0) * nkt - 1
    slot = step & 1

    def tile_copies(bb, tt, sl):
        k0 = (nkt - 1 - tt) * tk
        cps = []
        for src, dst in ((kc_hbm, kbuf_ref), (vc_hbm, vbuf_ref)):
            for h in range(SB_HEADS):
                cps.append(pltpu.make_async_copy(
                    src.at[bb, pl.ds(k0, tk), h, :],
                    dst.at[sl, :, pl.ds(h * SB_HEAD_DIM, SB_HEAD_DIM)], sem_ref.at[sl]))
        return cps

    @pl.when(step == 0)
    def _():
        for n_c, c in enumerate(tile_copies(b, kt, slot)):
            c.start(priority=n_c % 2)

    @pl.when(step < last)
    def _():
        nxt = step + 1
        nb_ = jnp.where(kt == nkt - 1, b + 1, b)
        nt_ = jnp.where(kt == nkt - 1, 0, kt + 1)
        for n_c, c in enumerate(tile_copies(nb_, nt_, nxt & 1)):
            c.start(priority=n_c % 2)

    @pl.when(kt == 0)
    def _():
        qbd, carry, acc = _own_tile(q_ref, ko_ref, vo_ref, tri_ref, ln)
        qbd_ref[...] = qbd
        carry_ref[...] = carry
        acc_ref[...] = acc

    for c in tile_copies(b, kt, slot):
        c.wait()

    qbd = qbd_ref[...]
    subs = list(reversed(range(tk // ATTN_TILE)))

    def tile(buf_ref, s):
        return buf_ref[slot, s * ATTN_TILE:(s + 1) * ATTN_TILE, :].astype(bf16)

    carries, accs = _sb_tiles([qbd] * len(subs), [tile(kbuf_ref, s) for s in subs],
                              [tile(vbuf_ref, s) for s in subs], tri_ref[...],
                              [carry_ref[...]], [acc_ref[...]], None, group=len(subs))
    carry = carries[0]
    acc = accs[0]
    carry_ref[...] = carry
    acc_ref[...] = acc

    @pl.when(kt == pl.num_programs(1) - 1)
    def _():
        _write_heads(o_ref, acc, ln)


def _attn_cached(qkvb, k_own, v_own, k_cache, v_cache, tri, nb, ln, tk):
    past = k_cache.shape[1]
    nkt = past // tk
    rows = SB_HEADS * ln
    return pl.pallas_call(
        functools.partial(_attn_cached_body, ln=ln, tk=tk, nkt=nkt),
        grid=(nb, nkt),
        in_specs=[
            pl.BlockSpec((ln, D_MODEL), lambda b, t: (b, 0)),
            pl.BlockSpec((128, D_MODEL), lambda b, t: (b, 0)),
            pl.BlockSpec((128, D_MODEL), lambda b, t: (b, 0)),
            pl.BlockSpec(memory_space=pl.ANY),
            pl.BlockSpec(memory_space=pl.ANY),
            pl.BlockSpec((ATTN_TILE, ATTN_TILE), lambda b, t: (0, 0)),
        ],
        out_specs=pl.BlockSpec((ln, D_MODEL), lambda b, t: (b, 0)),
        out_shape=jax.ShapeDtypeStruct((nb * ln, D_MODEL), bf16),
        scratch_shapes=[
            pltpu.VMEM((rows, D_MODEL), bf16),
            pltpu.VMEM((rows, 1), f32),
            pltpu.VMEM((rows, D_MODEL), f32),
            pltpu.VMEM((2, tk, D_MODEL), f32),
            pltpu.VMEM((2, tk, D_MODEL), f32),
            pltpu.SemaphoreType.DMA((2,)),
        ],
        compiler_params=_cparams(("arbitrary", "arbitrary")),
        name="attn_cached",
    )(qkvb, k_own, v_own, k_cache, v_cache, tri)


def _attn_own_body(q_ref, ko_ref, vo_ref, tri_ref, o_ref, *, ln):
    _, _, acc = _own_tile(q_ref, ko_ref, vo_ref, tri_ref, ln)
    _write_heads(o_ref, acc, ln)


def _attn_own(q, k_own, v_own, tri, ln):
    full = lambda i: (0, 0)
    return pl.pallas_call(
        functools.partial(_attn_own_body, ln=ln),
        grid=(1,),
        in_specs=[
            pl.BlockSpec((ln, D_MODEL), full),
            pl.BlockSpec((128, D_MODEL), full),
            pl.BlockSpec((128, D_MODEL), full),
            pl.BlockSpec((ATTN_TILE, ATTN_TILE), full),
        ],
        out_specs=pl.BlockSpec((ln, D_MODEL), full),
        out_shape=jax.ShapeDtypeStruct((ln, D_MODEL), bf16),
        compiler_params=_cparams(("arbitrary",)),
        name="attn_own",
    )(q, k_own, v_own, tri)


def kernel(x_prompt, x_sample, cache_pool, state_ssm_re, state_ssm_im, cache_k, cache_v, meta_tokens, ffn_norm, ffn_w_gate, ffn_w_up, ffn_w_down, mix_norm, ab_w_in, pool_w, pool_scale, ssm_a_re, ssm_a_im, ssm_log_dt, ssm_b_re, ssm_b_im, ssm_c_re, ssm_c_im, ssm_d, ssm_w_glu, ssm_b_glu, ab_w_out, sb_w_qkv, sb_w_out, final_norm):
    nb, seq, _ = x_prompt.shape
    db, dl, _ = x_sample.shape
    assert dl == N_META == CHUNK and seq % ATTN_TILE == 0
    n_small = db + 1
    rows_s = n_small * dl
    tm_p = TM_PROMPT

    def ffn(xp, xs, which, final_gamma=None):
        xs, wg, wu, wd = _ffn(xs, ffn_norm[which], ffn_w_gate, ffn_w_up, ffn_w_down, rows_s, FFN_TILE_F_SMALL,
                              final_gamma, which)
        xp = _ffn(xp, ffn_norm[which], wg, wu, wd, TM_FFN, FFN_TILE_F, final_gamma)
        return xp, xs

    w_in = ab_w_in[0].astype(bf16)
    w_out0 = ab_w_out[0].astype(bf16)
    w_pool = pool_w[0].astype(bf16)
    w_glu = ssm_w_glu[0].astype(bf16)
    w_qkv = sb_w_qkv[0].astype(bf16)
    w_out1 = sb_w_out[0].astype(bf16)
    mats = _ssm_matrices(ssm_a_re[0], ssm_a_im[0], ssm_log_dt[0], ssm_b_re[0], ssm_b_im[0],
                         ssm_c_re[0], ssm_c_im[0])
    ti = jnp.arange(ATTN_TILE)
    tri = (ti[:, None] > ti[None, :]).astype(bf16)

    xp = x_prompt.reshape(nb * seq, D_MODEL)
    xs = jnp.concatenate([x_sample.reshape(db * dl, D_MODEL), meta_tokens.astype(f32)], axis=0)

    xp, xs = ffn(xp, xs, (0, 0))
    up = _norm_proj(xp, mix_norm[0], w_in, tm_p, D_MODEL, False)
    us = _norm_proj(xs, mix_norm[0], w_in, rows_s, D_MODEL, False)

    pad_s = -(-n_small // 8) * 8
    zero_state = jnp.zeros((1, SSM_GROUPS, SSM_STATE), f32)
    h0r = _state_to_slices(jnp.concatenate([state_ssm_re[0], zero_state], axis=0), pad_s)
    h0i = _state_to_slices(jnp.concatenate([state_ssm_im[0], zero_state], axis=0), pad_s)
    us_pad = jnp.pad(us, ((0, (pad_s - n_small) * dl), (0, 0)))
    ys, hsr, hsi = _ssm_scan(us_pad, h0r, h0i, mats, 1, pad_s, 1)
    ys = ys[:rows_s]
    hsr = hsr[:, 0]
    hsi = hsi[:, 0]
    us3 = us.reshape(n_small, dl, D_MODEL)
    hist_s = jnp.concatenate([
        jnp.concatenate([jnp.zeros((db, 1, POOL_WIDTH), f32), cache_pool[0].astype(f32)], axis=1),
        jnp.zeros((1, N_META, POOL_WIDTH), f32)], axis=0)
    p0_s = jnp.concatenate([jnp.full((db, 1, POOL_GROUP), float(N_META), f32),
                            jnp.zeros((1, 1, POOL_GROUP), f32)], axis=0)
    xs = _mix0(us, ys, xs, hist_s, p0_s, w_pool, pool_scale[0], ssm_d[0], w_glu, ssm_b_glu[0], w_out0,
               n_small, dl)

    meta_r = hsr[:, db].reshape(SLICES, 1, 1, SLICE_STATE)
    meta_i = hsi[:, db].reshape(SLICES, 1, 1, SLICE_STATE)
    yp, hpr, hpi = _ssm_scan(up, meta_r, meta_i, mats, nb, 1, seq // CHUNK)
    sl_p = MIX0_SEG
    up4 = up.reshape(nb, seq // sl_p, sl_p, D_MODEL)
    meta_tail = jnp.broadcast_to(us3[db:, :, :POOL_WIDTH], (nb, N_META, POOL_WIDTH))
    hist_p = jnp.concatenate([meta_tail[:, None], up4[:, :-1, sl_p - N_META:, :POOL_WIDTH]], axis=1)
    hist_p = hist_p.reshape(nb * (seq // sl_p), N_META, POOL_WIDTH)
    p0_p = jnp.full((nb * (seq // sl_p), 1, POOL_GROUP), float(N_META), f32)
    xp = _mix0(up, yp, xp, hist_p, p0_p, w_pool, pool_scale[0], ssm_d[0], w_glu, ssm_b_glu[0], w_out0, 1, sl_p)

    xp, xs = ffn(xp, xs, (0, 1))

    xp, xs = ffn(xp, xs, (1, 0))
    qkv_s, qkvb_s = _norm_proj(xs, mix_norm[1], w_qkv, rows_s, 1024, True)
    qkvb_p, kp, vp = _qkv_prompt(xp, mix_norm[1], w_qkv, qkv_s[db * dl:], nb, seq)

    kcol = slice(D_MODEL, 2 * D_MODEL)
    vcol = slice(2 * D_MODEL, 3 * D_MODEL)
    pad_keys = lambda a: jnp.pad(a, ((0, 0), (0, 128 - dl), (0, 0))).reshape(-1, D_MODEL)
    kb3 = qkvb_s[:, kcol].reshape(n_small, dl, D_MODEL)
    vb3 = qkvb_s[:, vcol].reshape(n_small, dl, D_MODEL)
    k_own = pad_keys(kb3[:db])
    v_own = pad_keys(vb3[:db])
    k_meta = pad_keys(kb3[db:])
    v_meta = pad_keys(vb3[db:])

    o_sample = _attn_cached(qkvb_s, k_own, v_own, cache_k[0], cache_v[0], tri, db, dl, CACHE_TILE)
    o_meta = _attn_own(qkvb_s[db * dl:, :D_MODEL], k_meta, v_meta, tri, dl)
    o_prompt = _attn_prompt(qkvb_p, k_meta, v_meta, tri, nb, seq)
    xp = _proj_res(o_prompt, w_out1, xp, tm_p)
    xs = _proj_res(jnp.concatenate([o_sample, o_meta], axis=0), w_out1, xs, rows_s)

    yp_out, ys_out = ffn(xp, xs, (1, 1), final_norm)

    y_prompt = yp_out.reshape(nb, seq, D_MODEL)
    y_sample = ys_out[:db * dl].reshape(db, dl, D_MODEL)
    up3 = up.reshape(nb, seq, D_MODEL)
    pool_p = up3[:, seq - POOL_HIST:, :POOL_WIDTH][None]
    pool_s = us3[:db, dl - POOL_HIST:, :POOL_WIDTH][None]
    re_p = _state_from_slices(hpr[:, :, 0])[None]
    im_p = _state_from_slices(hpi[:, :, 0])[None]
    re_s = _state_from_slices(hsr[:, :db])[None]
    im_s = _state_from_slices(hsi[:, :db])[None]
    heads = lambda a, n: a.reshape(n, -1, SB_HEADS, SB_HEAD_DIM)
    k3 = qkv_s[:, kcol].reshape(n_small, dl, D_MODEL)
    v3 = qkv_s[:, vcol].reshape(n_small, dl, D_MODEL)
    k_p = kp[None]
    v_p = vp[None]
    k_s = heads(k3[:db], db)[None]
    v_s = heads(v3[:db], db)[None]
    return (y_prompt, y_sample, pool_p, pool_s, re_p, im_p, re_s, im_s, k_p, v_p, k_s, v_s)
```
